```python
import math
import jax, jax.numpy as jnp
from jax import lax
import numpy as np

D_MODEL = 1024
BATCH = 16
SEQ = 2048
DEPTH = 4

BRANCH_WIDTH = D_MODEL // 2
N_BRANCH = 3
POOL_WINDOWS = (2, 4, 8, 16)
POOL_GROUPS = len(POOL_WINDOWS)
POOL_WIDTH = BRANCH_WIDTH
POOL_GROUP_DIM = POOL_WIDTH // POOL_GROUPS
DA_HEAD_DIM = 64
DA_WIDTH = BRANCH_WIDTH
DA_HEADS = DA_WIDTH // (2 * DA_HEAD_DIM)
Q_BLOCK = 128
REL_BUCKETS = 32
REL_MAX_DIST = 128
RW_WIDTH = BRANCH_WIDTH
RW_HEAD_DIM = 64
RW_HEADS = RW_WIDTH // RW_HEAD_DIM
DECAY_RANK = 64
ICLR_RANK = 64
GATE_RANK = 128
RW_LOW = 2 * DECAY_RANK + 2 * ICLR_RANK + GATE_RANK
RW_SHIFT_WIDTH = 3 * RW_WIDTH + RW_LOW
RW_SPLITS = [RW_WIDTH, 2 * RW_WIDTH, 3 * RW_WIDTH,
             3 * RW_WIDTH + 2 * DECAY_RANK, 3 * RW_WIDTH + 2 * DECAY_RANK + 2 * ICLR_RANK]
GN_EPS = 64e-5
IN_SPLITS = [POOL_WIDTH, POOL_WIDTH + DA_WIDTH, POOL_WIDTH + 2 * DA_WIDTH,
             POOL_WIDTH + 3 * DA_WIDTH, POOL_WIDTH + 3 * DA_WIDTH + RW_SHIFT_WIDTH]
D_IN = POOL_WIDTH + 3 * DA_WIDTH + RW_SHIFT_WIDTH + N_BRANCH * D_MODEL
D_FF = 4 * D_MODEL
LN_EPS = 1e-5

kernel_name = 'hybrid_pool_diffattn_rwkv7_encoder'


def layer_norm(x, g, b):
    xf = x.astype(jnp.float32)
    mu = jnp.mean(xf, axis=-1, keepdims=True)
    var = jnp.mean(jnp.square(xf - mu), axis=-1, keepdims=True)
    return ((xf - mu) * lax.rsqrt(var + LN_EPS) * g + b).astype(x.dtype)


def multiscale_pool(u, pool_w, pool_scale):
    b, s, _ = u.shape
    uf = u.astype(jnp.float32)
    csum = jnp.concatenate([jnp.zeros((b, 1, POOL_WIDTH), jnp.float32), jnp.cumsum(uf, axis=1)], axis=1)
    t = jnp.arange(s)
    means = []
    for gi, win in enumerate(POOL_WINDOWS):
        left = win // 2
        right = win - 1 - left
        hi = jnp.minimum(t + right + 1, s)
        lo = jnp.maximum(t - left, 0)
        cg = csum[..., gi * POOL_GROUP_DIM:(gi + 1) * POOL_GROUP_DIM]
        cnt = (hi - lo).astype(jnp.float32)[None, :, None]
        means.append((cg[:, hi] - cg[:, lo]) / cnt)
    pooled = jnp.stack(means, axis=2)
    mixed = (pooled - uf.reshape(b, s, POOL_GROUPS, POOL_GROUP_DIM)).astype(u.dtype)
    y = jnp.einsum('bsgc,gcd->bsgd', mixed, pool_w).reshape(b, s, POOL_WIDTH)
    return y * pool_scale


def rel_bucket(rel):
    half = REL_BUCKETS // 2
    max_exact = half // 2
    n = jnp.abs(rel)
    nf = jnp.maximum(n, 1).astype(jnp.float32)
    large = max_exact + (jnp.log(nf / max_exact) / math.log(REL_MAX_DIST / max_exact)
                         * (half - max_exact)).astype(jnp.int32)
    large = jnp.minimum(large, half - 1)
    return jnp.where(rel > 0, half, 0) + jnp.where(n < max_exact, n, large)


def diff_attention(q, k, v, lam_vecs, subln_g, rel_table, layer_idx):
    b, s, _ = q.shape
    dtype = q.dtype
    q = q.reshape(b, s, DA_HEADS, 2, DA_HEAD_DIM) * (DA_HEAD_DIM ** -0.5)
    k = k.reshape(b, s, DA_HEADS, 2, DA_HEAD_DIM)
    v = v.reshape(b, s, DA_HEADS, 2 * DA_HEAD_DIM)
    lam_init = 0.8 - 0.6 * math.exp(-0.3 * layer_idx)
    lf = lam_vecs.astype(jnp.float32)
    lam = jnp.exp(jnp.sum(lf[0] * lf[1])) - jnp.exp(jnp.sum(lf[2] * lf[3])) + lam_init
    n_blk = s // Q_BLOCK
    qb = q.reshape(b, n_blk, Q_BLOCK, DA_HEADS, 2, DA_HEAD_DIM).transpose(1, 0, 2, 3, 4, 5)
    k_pos = jnp.arange(s)

    def block(args):
        q_blk, start = args
        logits = jnp.einsum('bqhmd,bkhmd->bhmqk', q_blk, k).astype(jnp.float32)
        rel = k_pos[None, :] - (start + jnp.arange(Q_BLOCK))[:, None]
        bias = rel_table[rel_bucket(rel)].astype(jnp.float32)
        logits = logits + bias.transpose(2, 0, 1)[None, :, None]
        p = jax.nn.softmax(logits, axis=-1)
        attn = p[:, :, 0] - lam * p[:, :, 1]
        return jnp.einsum('bhqk,bkhd->bqhd', attn.astype(v.dtype), v)

    starts = jnp.arange(n_blk) * Q_BLOCK
    o = lax.map(block, (qb, starts))
    o = o.transpose(1, 0, 2, 3, 4).reshape(b, s, DA_HEADS, 2 * DA_HEAD_DIM).astype(jnp.float32)
    o = o * lax.rsqrt(jnp.mean(o * o, axis=-1, keepdims=True) + 1e-5) * subln_g
    o = o * (1.0 - lam_init)
    return o.reshape(b, s, DA_WIDTH).astype(dtype)


def centred_token_shift(z, mu_prev, mu_next):
    prev = jnp.pad(z, ((0, 0), (1, 0), (0, 0)))[:, :-1]
    nxt = jnp.pad(z, ((0, 0), (0, 1), (0, 0)))[:, 1:]
    return z + mu_prev * (prev - z) + mu_next * (nxt - z)


def wkv7_scan(r, w, k, v, kk, a, reverse):
    b, s, h, n = r.shape

    def step(state, inp):
        r_t, w_t, k_t, v_t, kk_t, a_t = inp
        sa = jnp.einsum('bhij,bhj->bhi', state, -kk_t)
        state = (state * w_t[:, :, None, :] + sa[..., None] * (kk_t * a_t)[:, :, None, :]
                 + v_t[..., None] * k_t[:, :, None, :])
        return state, jnp.einsum('bhij,bhj->bhi', state, r_t)

    xs = tuple(jnp.moveaxis(t, 1, 0) for t in (r, w, k, v, kk, a))
    _, y = lax.scan(step, jnp.zeros((b, h, n, n), jnp.float32), xs, reverse=reverse)
    return jnp.moveaxis(y, 0, 1)


def rwkv7_bidir(z, mu_prev, mu_next, w0, w2, a0, a2, g2, k_k, k_a, r_k, gn_g, gn_b):
    out_dtype = z.dtype
    b, s, _ = z.shape
    z = centred_token_shift(z.astype(jnp.float32), mu_prev, mu_next)
    r, k, v, lw, la, lg = jnp.split(z, RW_SPLITS, axis=-1)

    def heads(t):
        return t.reshape(t.shape[:-1] + (RW_HEADS, RW_HEAD_DIM))

    lw = lw.reshape(b, s, 2, DECAY_RANK)
    la = la.reshape(b, s, 2, ICLR_RANK)
    w_log = -jax.nn.softplus(-(w0 + jnp.einsum('bsdr,drc->bsdc', jnp.tanh(lw), w2))) - 0.5
    decay = heads(jnp.exp(-jnp.exp(w_log)))
    iclr = heads(jax.nn.sigmoid(a0 + jnp.einsum('bsdr,drc->bsdc', la, a2)))
    gate = jnp.matmul(jax.nn.sigmoid(lg), g2)
    rh, kh, vh = heads(r), heads(k), heads(v)
    kk = kh * heads(k_k)
    kk = kk / jnp.maximum(jnp.sqrt(jnp.sum(kk * kk, axis=-1, keepdims=True)), 1e-12)
    k_a_h = heads(k_a)

    def direction(d, reverse):
        a_d = iclr[:, :, d]
        k_d = kh * (1.0 + (a_d - 1.0) * k_a_h)
        return wkv7_scan(rh, decay[:, :, d], k_d, vh, kk, a_d, reverse)

    y = direction(0, False) + direction(1, True)
    mu = jnp.mean(y, axis=-1, keepdims=True)
    var = jnp.mean(jnp.square(y - mu), axis=-1, keepdims=True)
    y = ((y - mu) * lax.rsqrt(var + GN_EPS)).reshape(b, s, RW_WIDTH) * gn_g + gn_b
    bonus = jnp.sum(rh * kh * heads(r_k), axis=-1, keepdims=True) * vh
    out = (y + bonus.reshape(b, s, RW_WIDTH)) * gate
    return out.astype(out_dtype)


def hybrid_mixer(h, w_in, pool_w, pool_scale, da_lambda, da_subln_g, rel_bias,
                 rw_mu_prev, rw_mu_next, rw_w0, rw_w2, rw_a0, rw_a2, rw_g2, rw_k_k, rw_k_a, rw_r_k,
                 rw_gn_g, rw_gn_b, w_branch, b_gate, w_out, layer_idx):
    b, s, _ = h.shape
    p = jnp.einsum('bsd,de->bse', h, w_in)
    u, q, k, v, zr, gl = jnp.split(p, IN_SPLITS, axis=-1)
    y_a = multiscale_pool(u, pool_w, pool_scale)
    y_b = diff_attention(q, k, v, da_lambda, da_subln_g, rel_bias, layer_idx)
    y_c = rwkv7_bidir(zr, rw_mu_prev, rw_mu_next, rw_w0, rw_w2, rw_a0, rw_a2, rw_g2,
                      rw_k_k, rw_k_a, rw_r_k, rw_gn_g, rw_gn_b)
    ys = jnp.stack([y_a, y_b, y_c], axis=2)
    branch = jnp.einsum('bsnc,ncd->bsnd', ys, w_branch)
    gates = jax.nn.sigmoid(gl.reshape(b, s, N_BRANCH, D_MODEL) + b_gate)
    merged = jnp.sum(gates * branch, axis=2)
    return jnp.einsum('bsd,de->bse', merged, w_out)


def setup_inputs(seed: int = 0) -> dict:
    key = jax.random.key(seed)
    ks = iter(jax.random.split(key, 40))
    L = DEPTH
    W = BRANCH_WIDTH
    beta = (8.0 * DEPTH) ** -0.25

    def nrm(shape, scale):
        return scale * jax.random.normal(next(ks), shape, jnp.float32)

    def unif(shape, lo, hi):
        return jax.random.uniform(next(ks), shape, jnp.float32, lo, hi)

    return {
        'x': nrm((BATCH, SEQ, D_MODEL), 1.0),
        'ln0_g': 1.0 + nrm((D_MODEL,), 0.05),
        'ln0_b': nrm((D_MODEL,), 0.01),
        'w_in': nrm((L, D_MODEL, D_IN), D_MODEL ** -0.5),
        'pool_w': nrm((L, POOL_GROUPS, POOL_GROUP_DIM, POOL_GROUP_DIM), POOL_GROUP_DIM ** -0.5),
        'pool_scale': 1.0 + nrm((L, POOL_WIDTH), 0.05),
        'da_lambda': nrm((L, 4, DA_HEAD_DIM), 0.1),
        'da_subln_g': 1.0 + nrm((L, 2 * DA_HEAD_DIM), 0.05),
        'rel_bias': nrm((REL_BUCKETS, DA_HEADS), 0.5),
        'rw_mu_prev': unif((L, RW_SHIFT_WIDTH), 0.0, 0.5),
        'rw_mu_next': unif((L, RW_SHIFT_WIDTH), 0.0, 0.5),
        'rw_w0': unif((L, 2, RW_WIDTH), -5.0, 2.0),
        'rw_w2': nrm((L, 2, DECAY_RANK, RW_WIDTH), 0.5 * DECAY_RANK ** -0.5),
        'rw_a0': nrm((L, 2, RW_WIDTH), 0.1),
        'rw_a2': nrm((L, 2, ICLR_RANK, RW_WIDTH), ICLR_RANK ** -0.5),
        'rw_g2': nrm((L, GATE_RANK, RW_WIDTH), GATE_RANK ** -0.5),
        'rw_k_k': 0.85 + nrm((L, RW_WIDTH), 0.05),
        'rw_k_a': 1.0 + nrm((L, RW_WIDTH), 0.05),
        'rw_r_k': nrm((L, RW_WIDTH), 0.1),
        'rw_gn_g': 1.0 + nrm((L, RW_WIDTH), 0.05),
        'rw_gn_b': nrm((L, RW_WIDTH), 0.01),
        'w_branch': nrm((L, N_BRANCH, W, D_MODEL), beta * W ** -0.5),
        'b_gate': nrm((L, N_BRANCH, D_MODEL), 0.01),
        'w_out': nrm((L, D_MODEL, D_MODEL), beta * D_MODEL ** -0.5),
        'ln1_g': 1.0 + nrm((L, D_MODEL), 0.05),
        'ln1_b': nrm((L, D_MODEL), 0.01),
        'w_up': nrm((L, D_MODEL, D_FF), D_MODEL ** -0.5),
        'w_down': nrm((L, D_FF, D_MODEL), beta * D_FF ** -0.5),
        'ln2_g': 1.0 + nrm((L, D_MODEL), 0.05),
        'ln2_b': nrm((L, D_MODEL), 0.01),
    }


def reference(x, ln0_g, ln0_b, w_in, pool_w, pool_scale, da_lambda, da_subln_g, rel_bias,
              rw_mu_prev, rw_mu_next, rw_w0, rw_w2, rw_a0, rw_a2, rw_g2, rw_k_k, rw_k_a, rw_r_k,
              rw_gn_g, rw_gn_b, w_branch, b_gate, w_out, ln1_g, ln1_b, w_up, w_down, ln2_g, ln2_b):
    alpha = (2.0 * DEPTH) ** 0.25
    x = layer_norm(x, ln0_g, ln0_b)
    for l in range(DEPTH):
        mix = hybrid_mixer(x, w_in[l], pool_w[l], pool_scale[l], da_lambda[l], da_subln_g[l], rel_bias,
                           rw_mu_prev[l], rw_mu_next[l], rw_w0[l], rw_w2[l], rw_a0[l], rw_a2[l], rw_g2[l],
                           rw_k_k[l], rw_k_a[l], rw_r_k[l], rw_gn_g[l], rw_gn_b[l],
                           w_branch[l], b_gate[l], w_out[l], l)
        x = layer_norm(alpha * x + mix, ln1_g[l], ln1_b[l])
        hid = jnp.square(jax.nn.relu(jnp.einsum('bsd,df->bsf', x, w_up[l])))
        x = layer_norm(alpha * x + jnp.einsum('bsf,fd->bsd', hid, w_down[l]), ln2_g[l], ln2_b[l])
    return x
```

```python
import functools
import math

import numpy as np
import jax
import jax.numpy as jnp
from jax import lax
from jax.experimental import pallas as pl
from jax.experimental.pallas import tpu as pltpu

D_MODEL = 1024
DEPTH = 4
BRANCH_WIDTH = 512
POOL_WINDOWS = (2, 4, 8, 16)
POOL_GROUP_DIM = 128
DA_HEAD_DIM = 64
DA_HEADS = 4
REL_BUCKETS = 32
REL_MAX_DIST = 128
RW_WIDTH = 512
RW_HEAD_DIM = 64
RW_HEADS = 8
DECAY_RANK = 64
ICLR_RANK = 64
GATE_RANK = 128
RW_SHIFT_WIDTH = 3 * RW_WIDTH + 2 * DECAY_RANK + 2 * ICLR_RANK + GATE_RANK
GN_EPS = 64e-5
D_FF = 4 * D_MODEL
LN_EPS = 1e-5
ALPHA = (2.0 * DEPTH) ** 0.25

VMEM_LIMIT_BYTES = 56 * 1024 * 1024

ROW_TILE = 512
ATTN_Q_TILE = 256
RW_PREP_TILE = 256
RW_CHUNK = 64
RW_INV_BASE = 8

F32 = jnp.float32
BF16 = jnp.bfloat16
HI = lax.Precision.HIGHEST
NT = (((1,), (1,)), ((), ()))
TN = (((0,), (0,)), ((), ()))


def _cparams(*sem):
    return pltpu.CompilerParams(dimension_semantics=sem, vmem_limit_bytes=VMEM_LIMIT_BYTES)


def _layer_norm(x, g, b):
    mu = jnp.mean(x, axis=-1, keepdims=True)
    xc = x - mu
    var = jnp.mean(xc * xc, axis=-1, keepdims=True)
    return xc * lax.rsqrt(var + LN_EPS) * g + b


def _ln0_kernel(x_ref, g_ref, b_ref, o_ref, ob_ref):
    y = _layer_norm(x_ref[...], g_ref[...], b_ref[...])
    o_ref[...] = y
    ob_ref[...] = y.astype(BF16)


def _ln0(x2, g, b):
    n = x2.shape[0]
    row = pl.BlockSpec((ROW_TILE, D_MODEL), lambda i: (i, 0))
    vec = pl.BlockSpec((1, D_MODEL), lambda i: (0, 0))
    return pl.pallas_call(
        _ln0_kernel,
        grid=(n // ROW_TILE,),
        in_specs=[row, vec, vec],
        out_specs=[row, row],
        out_shape=[jax.ShapeDtypeStruct((n, D_MODEL), F32), jax.ShapeDtypeStruct((n, D_MODEL), BF16)],
        compiler_params=_cparams("parallel"),
        name="ln0",
    )(x2, g.reshape(1, -1), b.reshape(1, -1))


def _proj_kernel(x_ref, w_ref, o_ref):
    o_ref[...] = jnp.dot(x_ref[...], w_ref[...], preferred_element_type=F32)


def _proj(xb, w, tn, name):
    n, k = xb.shape
    m = w.shape[1]
    return pl.pallas_call(
        _proj_kernel,
        grid=(n // ROW_TILE, m // tn),
        in_specs=[pl.BlockSpec((ROW_TILE, k), lambda i, j: (i, 0)),
                  pl.BlockSpec((k, tn), lambda i, j: (0, j))],
        out_specs=pl.BlockSpec((ROW_TILE, tn), lambda i, j: (i, j)),
        out_shape=jax.ShapeDtypeStruct((n, m), F32),
        compiler_params=_cparams("parallel", "arbitrary"),
        name=name,
    )(xb, w)


def _pool_kernel(u_ref, w_ref, s_ref, o_ref):
    s = u_ref.shape[1]
    t = lax.broadcasted_iota(jnp.int32, (s, POOL_GROUP_DIM), 0)
    for gi, win in enumerate(POOL_WINDOWS):
        left = win // 2
        right = win - 1 - left
        cols = slice(gi * POOL_GROUP_DIM, (gi + 1) * POOL_GROUP_DIM)
        x = u_ref[0, :, cols]
        acc = x
        for d in range(-left, right + 1):
            if d == 0:
                continue
            shifted = pltpu.roll(x, (-d) % s, axis=0)
            valid = jnp.logical_and(t + d >= 0, t + d < s)
            acc = acc + jnp.where(valid, shifted, 0.0)
        cnt = (jnp.minimum(t + right + 1, s) - jnp.maximum(t - left, 0)).astype(F32)
        mixed = acc / cnt - x
        y = jnp.dot(mixed.astype(BF16), w_ref[gi], preferred_element_type=F32)
        o_ref[0, :, cols] = (y * s_ref[:, cols]).astype(BF16)


def _pool(uqkv, pool_w, pool_scale):
    b, s, _ = uqkv.shape
    return pl.pallas_call(
        _pool_kernel,
        grid=(b,),
        in_specs=[pl.BlockSpec((1, s, BRANCH_WIDTH), lambda i: (i, 0, 0)),
                  pl.BlockSpec((len(POOL_WINDOWS), POOL_GROUP_DIM, POOL_GROUP_DIM), lambda i: (0, 0, 0)),
                  pl.BlockSpec((1, BRANCH_WIDTH), lambda i: (0, 0))],
        out_specs=pl.BlockSpec((1, s, BRANCH_WIDTH), lambda i: (i, 0, 0)),
        out_shape=jax.ShapeDtypeStruct((b, s, BRANCH_WIDTH), BF16),
        compiler_params=_cparams("parallel"),
        name="pool",
    )(uqkv, pool_w.astype(BF16), pool_scale.reshape(1, -1))


def _rel_bucket_np(rel):
    half = REL_BUCKETS // 2
    max_exact = half // 2
    n = np.abs(rel)
    nf = np.maximum(n, 1).astype(np.float64)
    large = max_exact + (np.log(nf / max_exact) / math.log(REL_MAX_DIST / max_exact)
                         * (half - max_exact)).astype(np.int32)
    large = np.minimum(large, half - 1)
    return (np.where(rel > 0, half, 0) + np.where(n < max_exact, n, large)).astype(np.int32)


def _bias_strip_kernel(tab_ref, bk_ref, o_ref):
    h = pl.program_id(0)
    bk = bk_ref[...]
    acc = jnp.zeros(bk.shape, F32)
    for j in range(REL_BUCKETS):
        acc = jnp.where(bk == j, tab_ref[j * DA_HEADS + h], acc)
    o_ref[0] = acc


def _bias_strip(rel_bias, s):
    tq = ATTN_Q_TILE
    w = 2 * s - tq
    rel = np.arange(w)[None, :] - np.arange(tq)[:, None] - (s - tq)
    bucket = jnp.asarray(_rel_bucket_np(rel))
    return pl.pallas_call(
        _bias_strip_kernel,
        grid=(DA_HEADS,),
        in_specs=[pl.BlockSpec(memory_space=pltpu.SMEM),
                  pl.BlockSpec((tq, w), lambda h: (0, 0))],
        out_specs=pl.BlockSpec((1, tq, w), lambda h: (h, 0, 0)),
        out_shape=jax.ShapeDtypeStruct((DA_HEADS, tq, w), F32),
        compiler_params=_cparams("arbitrary"),
        name="bias_strip",
    )(rel_bias.reshape(-1), bucket)


def _attn_kernel(lam0_ref, lamv_ref, q_ref, k_ref, v_ref, strip_ref, g_ref, o_ref):
    tq = q_ref.shape[1]
    s = k_ref.shape[1]
    qi = pl.program_id(2)
    nq = pl.num_programs(2)
    lam_init = lam0_ref[0]
    lv = lamv_ref[...]
    lam = (jnp.exp(jnp.sum(lv[0:1] * lv[1:2], axis=-1, keepdims=True))
           - jnp.exp(jnp.sum(lv[2:3] * lv[3:4], axis=-1, keepdims=True)) + lam_init)

    q = q_ref[0] * (DA_HEAD_DIM ** -0.5)
    lane = lax.broadcasted_iota(jnp.int32, q.shape, 1)
    k = k_ref[0].astype(BF16)
    v = v_ref[0].astype(BF16)
    off = pl.multiple_of((nq - 1 - qi) * tq, tq)
    bias = strip_ref[0, :, pl.ds(off, s)]

    def softmax_times_v(qm):
        logits = lax.dot_general(qm.astype(BF16), k, NT, preferred_element_type=F32) + bias
        m = jnp.max(logits, axis=-1, keepdims=True)
        e = jnp.exp(logits - m)
        l = jnp.sum(e, axis=-1, keepdims=True)
        return jnp.dot(e.astype(BF16), v, preferred_element_type=F32) / l

    o = (softmax_times_v(jnp.where(lane < DA_HEAD_DIM, q, 0.0))
         - lam * softmax_times_v(jnp.where(lane >= DA_HEAD_DIM, q, 0.0)))
    o = o * lax.rsqrt(jnp.mean(o * o, axis=-1, keepdims=True) + 1e-5) * g_ref[...]
    o_ref[0] = (o * (1.0 - lam_init)).astype(BF16)


def _diff_attention(uqkv, strip, da_lambda, subln_g, lam_init):
    b, s, _ = uqkv.shape
    tq = ATTN_Q_TILE
    hw = 2 * DA_HEAD_DIM
    q_blk0 = BRANCH_WIDTH // hw
    return pl.pallas_call(
        _attn_kernel,
        grid=(b, DA_HEADS, s // tq),
        in_specs=[pl.BlockSpec(memory_space=pltpu.SMEM),
                  pl.BlockSpec((4, DA_HEAD_DIM), lambda bi, h, qi: (0, 0)),
                  pl.BlockSpec((1, tq, hw), lambda bi, h, qi: (bi, qi, q_blk0 + h)),
                  pl.BlockSpec((1, s, hw), lambda bi, h, qi: (bi, 0, q_blk0 + DA_HEADS + h)),
                  pl.BlockSpec((1, s, hw), lambda bi, h, qi: (bi, 0, q_blk0 + 2 * DA_HEADS + h)),
                  pl.BlockSpec((1, tq, strip.shape[2]), lambda bi, h, qi: (h, 0, 0)),
                  pl.BlockSpec((1, hw), lambda bi, h, qi: (0, 0))],
        out_specs=pl.BlockSpec((1, tq, hw), lambda bi, h, qi: (bi, qi, h)),
        out_shape=jax.ShapeDtypeStruct((b, s, BRANCH_WIDTH), BF16),
        compiler_params=_cparams("parallel", "parallel", "arbitrary"),
        name="diff_attn",
    )(jnp.full((1,), lam_init, F32), da_lambda, uqkv, uqkv, uqkv, strip, subln_g.reshape(1, -1))


def _softplus(x):
    return jnp.maximum(x, 0.0) + jnp.log1p(jnp.exp(-jnp.abs(x)))


def _sigmoid(x):
    return 1.0 / (1.0 + jnp.exp(-x))


def _rw_prep_kernel(z_ref, zp_ref, zn_ref, mup_ref, mun_ref, w0_ref, w2_ref, a0_ref, a2_ref, g2_ref,
                    kk_w_ref, ka_w_ref, rk_w_ref, seg_ref,
                    r_ref, v_ref, kk_ref, lw_ref, kd_ref, be_ref, gate_ref, bonus_ref):
    i = pl.program_id(1)
    n = pl.num_programs(1)
    z = z_ref[0]
    ts = z.shape[0]
    t = lax.broadcasted_iota(jnp.int32, (ts, 1), 0)
    prev_row = jnp.where(i > 0, zp_ref[0, 7:8, :], 0.0)
    next_row = jnp.where(i < n - 1, zn_ref[0, 0:1, :], 0.0)
    prev = jnp.where(t == 0, prev_row, pltpu.roll(z, 1, axis=0))
    nxt = jnp.where(t == ts - 1, next_row, pltpu.roll(z, ts - 1, axis=0))
    z = z + mup_ref[...] * (prev - z) + mun_ref[...] * (nxt - z)

    w = RW_WIDTH
    r = z[:, 0:w]
    k = z[:, w:2 * w]
    v = z[:, 2 * w:3 * w]
    lw = z[:, 3 * w:3 * w + 2 * DECAY_RANK]
    la = z[:, 3 * w + 2 * DECAY_RANK:3 * w + 2 * DECAY_RANK + 2 * ICLR_RANK]
    lg = z[:, 3 * w + 2 * DECAY_RANK + 2 * ICLR_RANK:]

    w_log = -_softplus(-(w0_ref[...] + jnp.dot(jnp.tanh(lw), w2_ref[...], precision=HI))) - 0.5
    log_decay = -jnp.exp(w_log)
    iclr = _sigmoid(a0_ref[...] + jnp.dot(la, a2_ref[...], precision=HI))
    gate = jnp.dot(_sigmoid(lg), g2_ref[...], precision=HI)

    seg = seg_ref[...]
    kk = k * kk_w_ref[...]
    norm = jnp.sqrt(jnp.dot(kk * kk, seg, precision=HI))
    kk = kk / jnp.maximum(norm, 1e-12)
    bonus = jnp.dot(r * k * rk_w_ref[...], seg, precision=HI) * v

    r_ref[0] = r
    v_ref[0] = v
    kk_ref[0] = kk
    lw_ref[0] = log_decay
    gate_ref[0] = gate
    bonus_ref[0] = bonus
    ka = ka_w_ref[...]
    for d in range(2):
        a_d = iclr[:, d * w:(d + 1) * w]
        kd_ref[0, :, d * w:(d + 1) * w] = k * (1.0 + (a_d - 1.0) * ka)
        be_ref[0, :, d * w:(d + 1) * w] = kk * a_d


def _block_diag2(m):
    z = jnp.zeros_like(m[0])
    return jnp.concatenate([jnp.concatenate([m[0], z], axis=1), jnp.concatenate([z, m[1]], axis=1)], axis=0)


def _rw_prep(zr, mu_prev, mu_next, w0, w2, a0, a2, g2, k_k, k_a, r_k):
    b, s, c = zr.shape
    ts = RW_PREP_TILE
    w = RW_WIDTH
    seg = jnp.asarray(np.kron(np.eye(RW_HEADS, dtype=np.float32), np.ones((RW_HEAD_DIM, RW_HEAD_DIM), np.float32)))
    nblk8 = s // 8

    def vec(width):
        return pl.BlockSpec((1, width), lambda bi, i: (0, 0))

    def mat(rows, cols):
        return pl.BlockSpec((rows, cols), lambda bi, i: (0, 0))

    def out(width):
        return pl.BlockSpec((1, ts, width), lambda bi, i: (bi, i, 0))

    shapes = [jax.ShapeDtypeStruct((b, s, width), F32) for width in (w, w, w, 2 * w, 2 * w, 2 * w, w, w)]
    return pl.pallas_call(
        _rw_prep_kernel,
        grid=(b, s // ts),
        in_specs=[pl.BlockSpec((1, ts, c), lambda bi, i: (bi, i, 0)),
                  pl.BlockSpec((1, 8, c), lambda bi, i: (bi, jnp.maximum(i * (ts // 8) - 1, 0), 0)),
                  pl.BlockSpec((1, 8, c), lambda bi, i: (bi, jnp.minimum((i + 1) * (ts // 8), nblk8 - 1), 0)),
                  vec(c), vec(c), vec(2 * w), mat(2 * DECAY_RANK, 2 * w), vec(2 * w), mat(2 * ICLR_RANK, 2 * w),
                  mat(GATE_RANK, w), vec(w), vec(w), vec(w), mat(w, w)],
        out_specs=[out(w), out(w), out(w), out(2 * w), out(2 * w), out(2 * w), out(w), out(w)],
        out_shape=shapes,
        compiler_params=_cparams("parallel", "arbitrary"),
        name="rw_prep",
    )(zr, zr, zr, mu_prev.reshape(1, -1), mu_next.reshape(1, -1), w0.reshape(1, -1), _block_diag2(w2),
      a0.reshape(1, -1), _block_diag2(a2), g2, k_k.reshape(1, -1), k_a.reshape(1, -1), r_k.reshape(1, -1), seg)


def _unit_triangular_inverse(a, ti, si):
    def same_block(m):
        return (ti // m) == (si // m)

    eye = (ti == si).astype(F32)
    ad = jnp.where(same_block(RW_INV_BASE), a, 0.0)
    a2 = jnp.dot(ad, ad, precision=HI)
    inv = jnp.dot(eye + ad, eye + a2, precision=HI)
    m = 4
    while m < RW_INV_BASE:
        a2 = jnp.dot(a2, a2, precision=HI)
        inv = jnp.dot(inv, eye + a2, precision=HI)
        m *= 2
    m = RW_INV_BASE
    while m < a.shape[0]:
        off = jnp.where(jnp.logical_and(same_block(2 * m), jnp.logical_not(same_block(m))), a, 0.0)
        inv = inv + jnp.dot(inv, jnp.dot(off, inv, precision=HI), precision=HI)
        m *= 2
    return inv


def _rw_scan_kernel(r_ref, v_ref, kk_ref, lw_ref, kd_ref, be_ref, y_ref, state_ref):
    d = pl.program_id(1)
    c = pl.program_id(2)

    @pl.when(c == 0)
    def _():
        state_ref[...] = jnp.zeros_like(state_ref)

    n = RW_CHUNK
    ti = lax.broadcasted_iota(jnp.int32, (n, n), 0)
    si = lax.broadcasted_iota(jnp.int32, (n, n), 1)
    lag = (ti - si) * (1 - 2 * d)
    incl = lag >= 0
    strict = lag > 0

    lw = lw_ref[0]
    cum = jnp.dot(incl.astype(F32), lw, precision=HI)
    tot = jnp.sum(lw, axis=0, keepdims=True)
    e_neg = jnp.exp(-cum)
    a_all = -kk_ref[0] * jnp.exp(cum - lw)
    r_all = r_ref[0] * jnp.exp(cum)
    b_all = be_ref[0] * e_neg
    k_all = kd_ref[0] * e_neg
    v_all = v_ref[0]
    e_tot = jnp.exp(tot)

    for h in range(RW_HEADS):
        cols = slice(h * RW_HEAD_DIM, (h + 1) * RW_HEAD_DIM)
        a, rr, bb, kt, vv = a_all[:, cols], r_all[:, cols], b_all[:, cols], k_all[:, cols], v_all[:, cols]
        st = state_ref[h]
        a_ab = jnp.where(strict, lax.dot_general(a, bb, NT, precision=HI), 0.0)
        a_ak = jnp.where(strict, lax.dot_general(a, kt, NT, precision=HI), 0.0)
        a_rb = jnp.where(incl, lax.dot_general(rr, bb, NT, precision=HI), 0.0)
        a_rk = jnp.where(incl, lax.dot_general(rr, kt, NT, precision=HI), 0.0)
        inv = _unit_triangular_inverse(a_ab, ti, si)
        x = lax.dot_general(a, st, NT, precision=HI) + jnp.dot(a_ak, vv, precision=HI)
        u = jnp.dot(inv, x, precision=HI)
        y = (lax.dot_general(rr, st, NT, precision=HI) + jnp.dot(a_rb, u, precision=HI)
             + jnp.dot(a_rk, vv, precision=HI))
        y_ref[0, 0, :, cols] = y
        upd = lax.dot_general(u, bb, TN, precision=HI) + lax.dot_general(vv, kt, TN, precision=HI)
        state_ref[h] = (st + upd) * e_tot[:, cols]


def _rw_scan(r, v, kk, lw, kd, be):
    b, s, w = r.shape
    n = RW_CHUNK
    nc = s // n

    def tblk(d, c):
        return jnp.where(d == 0, c, nc - 1 - c)

    shared = pl.BlockSpec((1, n, w), lambda bi, d, c: (bi, tblk(d, c), 0))
    per_dir = pl.BlockSpec((1, n, w), lambda bi, d, c: (bi, tblk(d, c), d))
    return pl.pallas_call(
        _rw_scan_kernel,
        grid=(b, 2, nc),
        in_specs=[shared, shared, shared, per_dir, per_dir, per_dir],
        out_specs=pl.BlockSpec((1, 1, n, w), lambda bi, d, c: (d, bi, tblk(d, c), 0)),
        out_shape=jax.ShapeDtypeStruct((2, b, s, w), F32),
        scratch_shapes=[pltpu.VMEM((RW_HEADS, RW_HEAD_DIM, RW_HEAD_DIM), F32)],
        compiler_params=_cparams("parallel", "parallel", "arbitrary"),
        name="rw_scan",
    )(r, v, kk, lw, kd, be)


def _rw_out_kernel(y_ref, gate_ref, bonus_ref, g_ref, b_ref, seg_ref, o_ref):
    y = y_ref[0, 0] + y_ref[1, 0]
    seg = seg_ref[...] * (1.0 / RW_HEAD_DIM)
    mu = jnp.dot(y, seg, precision=HI)
    yc = y - mu
    var = jnp.dot(yc * yc, seg, precision=HI)
    yn = yc * lax.rsqrt(var + GN_EPS) * g_ref[...] + b_ref[...]
    o_ref[0] = ((yn + bonus_ref[0]) * gate_ref[0]).astype(BF16)


def _rw_out(y2, gate, bonus, gn_g, gn_b):
    _, b, s, w = y2.shape
    ts = ROW_TILE
    seg = jnp.asarray(np.kron(np.eye(RW_HEADS, dtype=np.float32), np.ones((RW_HEAD_DIM, RW_HEAD_DIM), np.float32)))
    tile = pl.BlockSpec((1, ts, w), lambda bi, i: (bi, i, 0))
    vec = pl.BlockSpec((1, w), lambda bi, i: (0, 0))
    return pl.pallas_call(
        _rw_out_kernel,
        grid=(b, s // ts),
        in_specs=[pl.BlockSpec((2, 1, ts, w), lambda bi, i: (0, bi, i, 0)), tile, tile, vec, vec,
                  pl.BlockSpec((w, w), lambda bi, i: (0, 0))],
        out_specs=tile,
        out_shape=jax.ShapeDtypeStruct((b, s, w), BF16),
        compiler_params=_cparams("parallel", "parallel"),
        name="rw_out",
    )(y2, gate, bonus, gn_g.reshape(1, -1), gn_b.reshape(1, -1), seg)


def _merge_kernel(ya_ref, yb_ref, yc_ref, gl_ref, wb_ref, bg_ref, wo_ref, x_ref, g_ref, b_ref, o_ref):
    merged = None
    for n, y_ref in enumerate((ya_ref, yb_ref, yc_ref)):
        cols = slice(n * D_MODEL, (n + 1) * D_MODEL)
        branch = jnp.dot(y_ref[...], wb_ref[n], preferred_element_type=F32)
        term = _sigmoid(gl_ref[:, cols] + bg_ref[:, cols]) * branch
        merged = term if merged is None else merged + term
    mix = jnp.dot(merged.astype(BF16), wo_ref[...], preferred_element_type=F32)
    o_ref[...] = _layer_norm(ALPHA * x_ref[...] + mix, g_ref[...], b_ref[...])


def _merge(ya, yb, yc, gl, w_branch, b_gate, w_out, x, g, b):
    n = x.shape[0]
    tm = ROW_TILE
    ytile = pl.BlockSpec((tm, BRANCH_WIDTH), lambda i: (i, 0))
    row = pl.BlockSpec((tm, D_MODEL), lambda i: (i, 0))
    vec = pl.BlockSpec((1, D_MODEL), lambda i: (0, 0))
    return pl.pallas_call(
        _merge_kernel,
        grid=(n // tm,),
        in_specs=[ytile, ytile, ytile,
                  pl.BlockSpec((tm, 3 * D_MODEL), lambda i: (i, 0)),
                  pl.BlockSpec((3, BRANCH_WIDTH, D_MODEL), lambda i: (0, 0, 0)),
                  pl.BlockSpec((1, 3 * D_MODEL), lambda i: (0, 0)),
                  pl.BlockSpec((D_MODEL, D_MODEL), lambda i: (0, 0)),
                  row, vec, vec],
        out_specs=row,
        out_shape=jax.ShapeDtypeStruct((n, D_MODEL), F32),
        compiler_params=_cparams("parallel"),
        name="merge",
    )(ya, yb, yc, gl, w_branch.astype(BF16), b_gate.reshape(1, -1), w_out.astype(BF16), x,
      g.reshape(1, -1), b.reshape(1, -1))


def _mlp_kernel(x_ref, wu_ref, wd_ref, g_ref, b_ref, o_ref, ob_ref, xb_ref, acc_ref):
    j = pl.program_id(1)

    @pl.when(j == 0)
    def _():
        xb_ref[...] = x_ref[...].astype(BF16)
        acc_ref[...] = jnp.zeros_like(acc_ref)

    hid = jnp.maximum(jnp.dot(xb_ref[...], wu_ref[...], preferred_element_type=F32), 0.0)
    acc_ref[...] += jnp.dot((hid * hid).astype(BF16), wd_ref[...], preferred_element_type=F32)

    @pl.when(j == pl.num_programs(1) - 1)
    def _():
        y = _layer_norm(ALPHA * x_ref[...] + acc_ref[...], g_ref[...], b_ref[...])
        o_ref[...] = y
        ob_ref[...] = y.astype(BF16)


def _mlp(x, w_up, w_down, g, b):
    n = x.shape[0]
    tm = ROW_TILE
    tf = D_MODEL
    row = pl.BlockSpec((tm, D_MODEL), lambda i, j: (i, 0))
    vec = pl.BlockSpec((1, D_MODEL), lambda i, j: (0, 0))
    return pl.pallas_call(
        _mlp_kernel,
        grid=(n // tm, D_FF // tf),
        in_specs=[row,
                  pl.BlockSpec((D_MODEL, tf), lambda i, j: (0, j)),
                  pl.BlockSpec((tf, D_MODEL), lambda i, j: (j, 0)),
                  vec, vec],
        out_specs=[row, row],
        out_shape=[jax.ShapeDtypeStruct((n, D_MODEL), F32), jax.ShapeDtypeStruct((n, D_MODEL), BF16)],
        scratch_shapes=[pltpu.VMEM((tm, D_MODEL), BF16), pltpu.VMEM((tm, D_MODEL), F32)],
        compiler_params=_cparams("parallel", "arbitrary"),
        name="mlp",
    )(x, w_up.astype(BF16), w_down.astype(BF16), g.reshape(1, -1), b.reshape(1, -1))


def kernel(x, ln0_g, ln0_b, w_in, pool_w, pool_scale, da_lambda, da_subln_g, rel_bias, rw_mu_prev, rw_mu_next, rw_w0, rw_w2, rw_a0, rw_a2, rw_g2, rw_k_k, rw_k_a, rw_r_k, rw_gn_g, rw_gn_b, w_branch, b_gate, w_out, ln1_g, ln1_b, w_up, w_down, ln2_g, ln2_b):
    b, s, dm = x.shape
    n = b * s
    assert dm == D_MODEL and n % ROW_TILE == 0 and s % ATTN_Q_TILE == 0 and s % ROW_TILE == 0
    c_uqkv = 4 * BRANCH_WIDTH
    c_zr = c_uqkv + RW_SHIFT_WIDTH

    strip = _bias_strip(rel_bias, s)
    xf, xb = _ln0(x.reshape(n, dm), ln0_g, ln0_b)
    for l in range(DEPTH):
        wl = w_in[l].astype(BF16)
        uqkv = _proj(xb, wl[:, :c_uqkv], 1024, "proj_uqkv").reshape(b, s, c_uqkv)
        zr = _proj(xb, wl[:, c_uqkv:c_zr], RW_SHIFT_WIDTH, "proj_zr").reshape(b, s, RW_SHIFT_WIDTH)
        gl = _proj(xb, wl[:, c_zr:], 1024, "proj_gate")

        y_a = _pool(uqkv, pool_w[l], pool_scale[l])
        lam_init = 0.8 - 0.6 * math.exp(-0.3 * l)
        y_b = _diff_attention(uqkv, strip, da_lambda[l], da_subln_g[l], lam_init)
        r, v, kk, lw, kd, be, gate, bonus = _rw_prep(
            zr, rw_mu_prev[l], rw_mu_next[l], rw_w0[l], rw_w2[l], rw_a0[l], rw_a2[l], rw_g2[l],
            rw_k_k[l], rw_k_a[l], rw_r_k[l])
        y2 = _rw_scan(r, v, kk, lw, kd, be)
        y_c = _rw_out(y2, gate, bonus, rw_gn_g[l], rw_gn_b[l])

        xf = _merge(y_a.reshape(n, -1), y_b.reshape(n, -1), y_c.reshape(n, -1), gl,
                    w_branch[l], b_gate[l], w_out[l], xf, ln1_g[l], ln1_b[l])
        xf, xb = _mlp(xf, w_up[l], w_down[l], ln2_g[l], ln2_b[l])
    return xf.reshape(b, s, dm)
```

```python
import functools
import math

import numpy as np
import jax
import jax.numpy as jnp
from jax import lax
from jax.experimental import pallas as pl
from jax.experimental.pallas import tpu as pltpu

D_MODEL = 1024
DEPTH = 4
BRANCH_WIDTH = 512
POOL_WINDOWS = (2, 4, 8, 16)
POOL_GROUP_DIM = 128
DA_HEAD_DIM = 64
DA_HEADS = 4
REL_BUCKETS = 32
REL_MAX_DIST = 128
RW_WIDTH = 512
RW_HEAD_DIM = 64
RW_HEADS = 8
DECAY_RANK = 64
ICLR_RANK = 64
GATE_RANK = 128
RW_SHIFT_WIDTH = 3 * RW_WIDTH + 2 * DECAY_RANK + 2 * ICLR_RANK + GATE_RANK
GN_EPS = 64e-5
D_FF = 4 * D_MODEL
LN_EPS = 1e-5
ALPHA = (2.0 * DEPTH) ** 0.25

VMEM_LIMIT_BYTES = 56 * 1024 * 1024

ROW_TILE = 512
ATTN_Q_TILE = 256
RW_PREP_TILE = 256
RW_CHUNK = 64
RW_INV_BASE = 8
INV_PASSES = 1
MIX_PASSES = 1

F32 = jnp.float32
BF16 = jnp.bfloat16
HI = lax.Precision.HIGHEST
NN = (((1,), (0,)), ((), ()))
NT =(((1,), (1,)), ((), ()))
TN = (((0,), (0,)), ((), ()))


def _cparams(*sem):
    return pltpu.CompilerParams(dimension_semantics=sem, vmem_limit_bytes=VMEM_LIMIT_BYTES)


def _layer_norm(x, g, b):
    mu = jnp.mean(x, axis=-1, keepdims=True)
    xc = x - mu
    var = jnp.mean(xc * xc, axis=-1, keepdims=True)
    return xc * lax.rsqrt(var + LN_EPS) * g + b


def _ln0_kernel(x_ref, g_ref, b_ref, o_ref, ob_ref):
    y = _layer_norm(x_ref[...], g_ref[...], b_ref[...])
    o_ref[...] = y
    ob_ref[...] = y.astype(BF16)


def _ln0(x2, g, b):
    n = x2.shape[0]
    row = pl.BlockSpec((ROW_TILE, D_MODEL), lambda i: (i, 0))
    vec = pl.BlockSpec((1, D_MODEL), lambda i: (0, 0))
    return pl.pallas_call(
        _ln0_kernel,
        grid=(n // ROW_TILE,),
        in_specs=[row, vec, vec],
        out_specs=[row, row],
        out_shape=[jax.ShapeDtypeStruct((n, D_MODEL), F32), jax.ShapeDtypeStruct((n, D_MODEL), BF16)],
        compiler_params=_cparams("parallel"),
        name="ln0",
    )(x2, g.reshape(1, -1), b.reshape(1, -1))


def _proj_kernel(x_ref, w_ref, o_ref):
    o_ref[...] = jnp.dot(x_ref[...], w_ref[...], preferred_element_type=F32)


def _proj(xb, w, tn, name):
    n, k = xb.shape
    m = w.shape[1]
    return pl.pallas_call(
        _proj_kernel,
        grid=(n // ROW_TILE, m // tn),
        in_specs=[pl.BlockSpec((ROW_TILE, k), lambda i, j: (i, 0)),
                  pl.BlockSpec((k, tn), lambda i, j: (0, j))],
        out_specs=pl.BlockSpec((ROW_TILE, tn), lambda i, j: (i, j)),
        out_shape=jax.ShapeDtypeStruct((n, m), F32),
        compiler_params=_cparams("parallel", "arbitrary"),
        name=name,
    )(xb, w)


def _pool_kernel(u_ref, w_ref, s_ref, o_ref):
    s = u_ref.shape[1]
    t = lax.broadcasted_iota(jnp.int32, (s, POOL_GROUP_DIM), 0)
    for gi, win in enumerate(POOL_WINDOWS):
        left = win // 2
        right = win - 1 - left
        cols = slice(gi * POOL_GROUP_DIM, (gi + 1) * POOL_GROUP_DIM)
        x = u_ref[0, :, cols]
        acc = x
        for d in range(-left, right + 1):
            if d == 0:
                continue
            shifted = pltpu.roll(x, (-d) % s, axis=0)
            valid = jnp.logical_and(t + d >= 0, t + d < s)
            acc = acc + jnp.where(valid, shifted, 0.0)
        cnt = (jnp.minimum(t + right + 1, s) - jnp.maximum(t - left, 0)).astype(F32)
        mixed = acc / cnt - x
        y = jnp.dot(mixed.astype(BF16), w_ref[gi], preferred_element_type=F32)
        o_ref[0, :, cols] = (y * s_ref[:, cols]).astype(BF16)


def _pool(uqkv, pool_w, pool_scale):
    b, s, _ = uqkv.shape
    return pl.pallas_call(
        _pool_kernel,
        grid=(b,),
        in_specs=[pl.BlockSpec((1, s, BRANCH_WIDTH), lambda i: (i, 0, 0)),
                  pl.BlockSpec((len(POOL_WINDOWS), POOL_GROUP_DIM, POOL_GROUP_DIM), lambda i: (0, 0, 0)),
                  pl.BlockSpec((1, BRANCH_WIDTH), lambda i: (0, 0))],
        out_specs=pl.BlockSpec((1, s, BRANCH_WIDTH), lambda i: (i, 0, 0)),
        out_shape=jax.ShapeDtypeStruct((b, s, BRANCH_WIDTH), BF16),
        compiler_params=_cparams("parallel"),
        name="pool",
    )(uqkv, pool_w.astype(BF16), pool_scale.reshape(1, -1))


def _rel_bucket_np(rel):
    half = REL_BUCKETS // 2
    max_exact = half // 2
    n = np.abs(rel)
    nf = np.maximum(n, 1).astype(np.float64)
    large = max_exact + (np.log(nf / max_exact) / math.log(REL_MAX_DIST / max_exact)
                         * (half - max_exact)).astype(np.int32)
    large = np.minimum(large, half - 1)
    return (np.where(rel > 0, half, 0) + np.where(n < max_exact, n, large)).astype(np.int32)


def _bias_strip_kernel(tab_ref, bk_ref, o_ref):
    h = pl.program_id(0)
    bk = bk_ref[...]
    acc = jnp.zeros(bk.shape, F32)
    for j in range(REL_BUCKETS):
        acc = jnp.where(bk == j, tab_ref[j * DA_HEADS + h], acc)
    o_ref[0] = acc


def _bias_strip(rel_bias, s):
    tq = ATTN_Q_TILE
    w = 2 * s - tq
    rel = np.arange(w)[None, :] - np.arange(tq)[:, None] - (s - tq)
    bucket = jnp.asarray(_rel_bucket_np(rel))
    return pl.pallas_call(
        _bias_strip_kernel,
        grid=(DA_HEADS,),
        in_specs=[pl.BlockSpec(memory_space=pltpu.SMEM),
                  pl.BlockSpec((tq, w), lambda h: (0, 0))],
        out_specs=pl.BlockSpec((1, tq, w), lambda h: (h, 0, 0)),
        out_shape=jax.ShapeDtypeStruct((DA_HEADS, tq, w), F32),
        compiler_params=_cparams("arbitrary"),
        name="bias_strip",
    )(rel_bias.reshape(-1), bucket)


def _attn_kernel(lam0_ref, lamv_ref, q_ref, k_ref, v_ref, strip_ref, g_ref, o_ref):
    tq = q_ref.shape[1]
    s = k_ref.shape[1]
    qi = pl.program_id(2)
    nq = pl.num_programs(2)
    lam_init = lam0_ref[0]
    lv = lamv_ref[...]
    lam = (jnp.exp(jnp.sum(lv[0:1] * lv[1:2], axis=-1, keepdims=True))
           - jnp.exp(jnp.sum(lv[2:3] * lv[3:4], axis=-1, keepdims=True)) + lam_init)

    q = q_ref[0] * (DA_HEAD_DIM ** -0.5)
    lane = lax.broadcasted_iota(jnp.int32, q.shape, 1)
    k = k_ref[0].astype(BF16)
    v = v_ref[0].astype(BF16)
    off = pl.multiple_of((nq - 1 - qi) * tq, tq)
    bias = strip_ref[0, :, pl.ds(off, s)]

    def softmax_times_v(qm):
        logits = lax.dot_general(qm.astype(BF16), k, NT, preferred_element_type=F32) + bias
        m = jnp.max(logits, axis=-1, keepdims=True)
        e = jnp.exp(logits - m)
        l = jnp.sum(e, axis=-1, keepdims=True)
        return jnp.dot(e.astype(BF16), v, preferred_element_type=F32) / l

    o = (softmax_times_v(jnp.where(lane < DA_HEAD_DIM, q, 0.0))
         - lam * softmax_times_v(jnp.where(lane >= DA_HEAD_DIM, q, 0.0)))
    o = o * lax.rsqrt(jnp.mean(o * o, axis=-1, keepdims=True) + 1e-5) * g_ref[...]
    o_ref[0] = (o * (1.0 - lam_init)).astype(BF16)


def _diff_attention(uqkv, strip, da_lambda, subln_g, lam_init):
    b, s, _ = uqkv.shape
    tq = ATTN_Q_TILE
    hw = 2 * DA_HEAD_DIM
    q_blk0 = BRANCH_WIDTH // hw
    return pl.pallas_call(
        _attn_kernel,
        grid=(b, DA_HEADS, s // tq),
        in_specs=[pl.BlockSpec(memory_space=pltpu.SMEM),
                  pl.BlockSpec((4, DA_HEAD_DIM), lambda bi, h, qi: (0, 0)),
                  pl.BlockSpec((1, tq, hw), lambda bi, h, qi: (bi, qi, q_blk0 + h)),
                  pl.BlockSpec((1, s, hw), lambda bi, h, qi: (bi, 0, q_blk0 + DA_HEADS + h)),
                  pl.BlockSpec((1, s, hw), lambda bi, h, qi: (bi, 0, q_blk0 + 2 * DA_HEADS + h)),
                  pl.BlockSpec((1, tq, strip.shape[2]), lambda bi, h, qi: (h, 0, 0)),
                  pl.BlockSpec((1, hw), lambda bi, h, qi: (0, 0))],
        out_specs=pl.BlockSpec((1, tq, hw), lambda bi, h, qi: (bi, qi, h)),
        out_shape=jax.ShapeDtypeStruct((b, s, BRANCH_WIDTH), BF16),
        compiler_params=_cparams("parallel", "parallel", "arbitrary"),
        name="diff_attn",
    )(jnp.full((1,), lam_init, F32), da_lambda, uqkv, uqkv, uqkv, strip, subln_g.reshape(1, -1))


def _softplus(x):
    return jnp.maximum(x, 0.0) + jnp.log1p(jnp.exp(-jnp.abs(x)))


def _sigmoid(x):
    return 1.0 / (1.0 + jnp.exp(-x))


def _rw_prep_kernel(z_ref, zp_ref, zn_ref, mup_ref, mun_ref, w0_ref, w2_ref, a0_ref, a2_ref, g2_ref,
                    kk_w_ref, ka_w_ref, rk_w_ref, seg_ref,
                    r_ref, v_ref, kk_ref, lw_ref, kd_ref, be_ref, gate_ref, bonus_ref):
    i = pl.program_id(1)
    n = pl.num_programs(1)
    z = z_ref[0]
    ts = z.shape[0]
    t = lax.broadcasted_iota(jnp.int32, (ts, 1), 0)
    prev_row = jnp.where(i > 0, zp_ref[0, 7:8, :], 0.0)
    next_row = jnp.where(i < n - 1, zn_ref[0, 0:1, :], 0.0)
    prev = jnp.where(t == 0, prev_row, pltpu.roll(z, 1, axis=0))
    nxt = jnp.where(t == ts - 1, next_row, pltpu.roll(z, ts - 1, axis=0))
    z = z + mup_ref[...] * (prev - z) + mun_ref[...] * (nxt - z)

    w = RW_WIDTH
    r = z[:, 0:w]
    k = z[:, w:2 * w]
    v = z[:, 2 * w:3 * w]
    lw = z[:, 3 * w:3 * w + 2 * DECAY_RANK]
    la = z[:, 3 * w + 2 * DECAY_RANK:3 * w + 2 * DECAY_RANK + 2 * ICLR_RANK]
    lg = z[:, 3 * w + 2 * DECAY_RANK + 2 * ICLR_RANK:]

    w_log = -_softplus(-(w0_ref[...] + jnp.dot(jnp.tanh(lw), w2_ref[...], precision=HI))) - 0.5
    log_decay = -jnp.exp(w_log)
    iclr = _sigmoid(a0_ref[...] + jnp.dot(la, a2_ref[...], precision=HI))
    gate = jnp.dot(_sigmoid(lg), g2_ref[...], precision=HI)

    seg = seg_ref[...]
    kk = k * kk_w_ref[...]
    norm = jnp.sqrt(jnp.dot(kk * kk, seg, precision=HI))
    kk = kk / jnp.maximum(norm, 1e-12)
    bonus = jnp.dot(r * k * rk_w_ref[...], seg, precision=HI) * v

    r_ref[0] = r
    v_ref[0] = v
    kk_ref[0] = kk
    lw_ref[0] = log_decay
    gate_ref[0] = gate
    bonus_ref[0] = bonus
    ka = ka_w_ref[...]
    for d in range(2):
        a_d = iclr[:, d * w:(d + 1) * w]
        kd_ref[0, :, d * w:(d + 1) * w] = k * (1.0 + (a_d - 1.0) * ka)
        be_ref[0, :, d * w:(d + 1) * w] = kk * a_d


def _block_diag2(m):
    z = jnp.zeros_like(m[0])
    return jnp.concatenate([jnp.concatenate([m[0], z], axis=1), jnp.concatenate([z, m[1]], axis=1)], axis=0)


def _rw_prep(zr, mu_prev, mu_next, w0, w2, a0, a2, g2, k_k, k_a, r_k):
    b, s, c = zr.shape
    ts = RW_PREP_TILE
    w = RW_WIDTH
    seg = jnp.asarray(np.kron(np.eye(RW_HEADS, dtype=np.float32), np.ones((RW_HEAD_DIM, RW_HEAD_DIM), np.float32)))
    nblk8 = s // 8

    def vec(width):
        return pl.BlockSpec((1, width), lambda bi, i: (0, 0))

    def mat(rows, cols):
        return pl.BlockSpec((rows, cols), lambda bi, i: (0, 0))

    def out(width):
        return pl.BlockSpec((1, ts, width), lambda bi, i: (bi, i, 0))

    shapes = [jax.ShapeDtypeStruct((b, s, width), F32) for width in (w, w, w, 2 * w, 2 * w, 2 * w, w, w)]
    return pl.pallas_call(
        _rw_prep_kernel,
        grid=(b, s // ts),
        in_specs=[pl.BlockSpec((1, ts, c), lambda bi, i: (bi, i, 0)),
                  pl.BlockSpec((1, 8, c), lambda bi, i: (bi, jnp.maximum(i * (ts // 8) - 1, 0), 0)),
                  pl.BlockSpec((1, 8, c), lambda bi, i: (bi, jnp.minimum((i + 1) * (ts // 8), nblk8 - 1), 0)),
                  vec(c), vec(c), vec(2 * w), mat(2 * DECAY_RANK, 2 * w), vec(2 * w), mat(2 * ICLR_RANK, 2 * w),
                  mat(GATE_RANK, w), vec(w), vec(w), vec(w), mat(w, w)],
        out_specs=[out(w), out(w), out(w), out(2 * w), out(2 * w), out(2 * w), out(w), out(w)],
        out_shape=shapes,
        compiler_params=_cparams("parallel", "arbitrary"),
        name="rw_prep",
    )(zr, zr, zr, mu_prev.reshape(1, -1), mu_next.reshape(1, -1), w0.reshape(1, -1), _block_diag2(w2),
      a0.reshape(1, -1), _block_diag2(a2), g2, k_k.reshape(1, -1), k_a.reshape(1, -1), r_k.reshape(1, -1), seg)


def _split_bf16(x, terms):
    parts = []
    for _ in range(terms):
        p = x.astype(BF16)
        parts.append(p)
        x = x - p.astype(F32)
    return parts


def _mm(a, b, dims, passes):
    if passes == 1:
        return lax.dot_general(a.astype(BF16), b.astype(BF16), dims, preferred_element_type=F32)
    assert passes == 3
    ah, al = _split_bf16(a, 2)
    bh, bl = _split_bf16(b, 2)
    return (lax.dot_general(ah, bh, dims, preferred_element_type=F32)
            + lax.dot_general(ah, bl, dims, preferred_element_type=F32)
            + lax.dot_general(al, bh, dims, preferred_element_type=F32))


def _unit_triangular_inverses(mats):
    n = mats[0].shape[0]
    ti = lax.broadcasted_iota(jnp.int32, (n, n), 0)
    si = lax.broadcasted_iota(jnp.int32, (n, n), 1)

    def same_block(m):
        return (ti // m) == (si // m)

    mm = functools.partial(_mm, dims=NN, passes=INV_PASSES)
    eye = (ti == si).astype(F32)
    base = same_block(RW_INV_BASE)
    diag = [jnp.where(base, a, 0.0) for a in mats]
    power = [mm(a, a) for a in diag]
    inv = [mm(eye + a, eye + p) for a, p in zip(diag, power)]
    m = 4
    while m < RW_INV_BASE:
        power = [mm(p, p) for p in power]
        inv = [mm(t, eye + p) for t, p in zip(inv, power)]
        m *= 2
    m = RW_INV_BASE
    while m < n:
        ring = jnp.logical_and(same_block(2 * m), jnp.logical_not(same_block(m)))
        tmp = [mm(jnp.where(ring, a, 0.0), t) for a, t in zip(mats, inv)]
        inv = [t + mm(t, x) for t, x in zip(inv, tmp)]
        m *= 2
    return inv


def _rw_scan_kernel(rf_ref, vf_ref, kkf_ref, lwf_ref, kdf_ref, bef_ref,
                    rb_ref, vb_ref, kkb_ref, lwb_ref, kdb_ref, beb_ref, yf_ref, yb_ref, state_ref):
    @pl.when(pl.program_id(1) == 0)
    def _():
        state_ref[...] = jnp.zeros_like(state_ref)

    n = RW_CHUNK
    hd = RW_HEAD_DIM
    ti = lax.broadcasted_iota(jnp.int32, (2 * n, 2 * n), 0)
    si = lax.broadcasted_iota(jnp.int32, (2 * n, 2 * n), 1)
    zeros = jnp.zeros((n, hd), F32)

    chains = []
    for d, (r_ref, v_ref, kk_ref, lw_ref, kd_ref, be_ref) in enumerate((
            (rf_ref, vf_ref, kkf_ref, lwf_ref, kdf_ref, bef_ref),
            (rb_ref, vb_ref, kkb_ref, lwb_ref, kdb_ref, beb_ref))):
        lag = (ti % n - si % n) * (1 - 2 * d)
        keep = lag >= jnp.where(ti < n, 1, 0)
        lw = lw_ref[0]
        tri = (lag[:n, :n] >= 0).astype(BF16)
        cum = sum(jnp.dot(tri, part, preferred_element_type=F32) for part in _split_bf16(lw, 3))
        e_neg = jnp.exp(-cum)
        ar_all = jnp.concatenate([-kk_ref[0] * jnp.exp(cum - lw), r_ref[0] * jnp.exp(cum)], axis=0)
        bk_all = jnp.concatenate([be_ref[0] * e_neg, kd_ref[0] * e_neg], axis=0)
        v_all = v_ref[0]
        e_tot = jnp.exp(jnp.sum(lw, axis=0, keepdims=True))
        for h in range(RW_HEADS):
            cols = slice(h * hd, (h + 1) * hd)
            chains.append(dict(d=d, h=h, cols=cols, keep=keep, ar=ar_all[:, cols], bk=bk_all[:, cols],
                               v=v_all[:, cols], decay=e_tot[:, cols], st=state_ref[d, h]))

    for ch in chains:
        ch["m"] = jnp.where(ch["keep"], _mm(ch["ar"], ch["bk"], NT, MIX_PASSES), 0.0)
    for ch in chains:
        ch["xy"] = (_mm(ch["ar"], ch["st"], NT, MIX_PASSES)
                    + _mm(ch["m"], jnp.concatenate([zeros, ch["v"]], axis=0), NN, MIX_PASSES))
    invs = _unit_triangular_inverses([ch["m"][:n, :n] for ch in chains])
    for ch, inv in zip(chains, invs):
        ch["u"] = _mm(inv, ch["xy"][:n], NN, INV_PASSES)
    for ch in chains:
        y = ch["xy"][n:] + _mm(ch["m"][n:], jnp.concatenate([ch["u"], zeros], axis=0), NN, MIX_PASSES)
        (yf_ref, yb_ref)[ch["d"]][0, :, ch["cols"]] = y
    for ch in chains:
        upd = _mm(jnp.concatenate([ch["u"], ch["v"]], axis=0), ch["bk"], TN, MIX_PASSES)
        state_ref[ch["d"], ch["h"]] = (ch["st"] + upd) * ch["decay"]


def _rw_scan(r, v, kk, lw, kd, be):
    b, s, w = r.shape
    n = RW_CHUNK
    nc = s // n

    def spec(d, col):
        if d == 0:
            return pl.BlockSpec((1, n, w), lambda bi, c: (bi, c, col))
        return pl.BlockSpec((1, n, w), lambda bi, c: (bi, nc - 1 - c, col))

    in_specs = []
    for d in range(2):
        in_specs += [spec(d, 0), spec(d, 0), spec(d, 0), spec(d, d), spec(d, d), spec(d, d)]
    return pl.pallas_call(
        _rw_scan_kernel,
        grid=(b, nc),
        in_specs=in_specs,
        out_specs=[spec(0, 0), spec(1, 0)],
        out_shape=[jax.ShapeDtypeStruct((b, s, w), F32)] * 2,
        scratch_shapes=[pltpu.VMEM((2, RW_HEADS, RW_HEAD_DIM, RW_HEAD_DIM), F32)],
        compiler_params=_cparams("parallel", "arbitrary"),
        name="rw_scan",
    )(r, v, kk, lw, kd, be, r, v, kk, lw, kd, be)


def _rw_out_kernel(yf_ref, yb_ref, gate_ref, bonus_ref, g_ref, b_ref, seg_ref, o_ref):
    y = yf_ref[0] + yb_ref[0]
    seg = seg_ref[...] * (1.0 / RW_HEAD_DIM)
    mu = jnp.dot(y, seg, precision=HI)
    yc = y - mu
    var = jnp.dot(yc * yc, seg, precision=HI)
    yn = yc * lax.rsqrt(var + GN_EPS) * g_ref[...] + b_ref[...]
    o_ref[0] = ((yn + bonus_ref[0]) * gate_ref[0]).astype(BF16)


def _rw_out(yf, yb, gate, bonus, gn_g, gn_b):
    b, s, w = yf.shape
    ts = ROW_TILE
    seg = jnp.asarray(np.kron(np.eye(RW_HEADS, dtype=np.float32), np.ones((RW_HEAD_DIM, RW_HEAD_DIM), np.float32)))
    tile = pl.BlockSpec((1, ts, w), lambda bi, i: (bi, i, 0))
    vec = pl.BlockSpec((1, w), lambda bi, i: (0, 0))
    return pl.pallas_call(
        _rw_out_kernel,
        grid=(b, s // ts),
        in_specs=[tile, tile, tile, tile, vec, vec,
                  pl.BlockSpec((w, w), lambda bi, i: (0, 0))],
        out_specs=tile,
        out_shape=jax.ShapeDtypeStruct((b, s, w), BF16),
        compiler_params=_cparams("parallel", "parallel"),
        name="rw_out",
    )(yf, yb, gate, bonus, gn_g.reshape(1, -1), gn_b.reshape(1, -1), seg)


def _merge_kernel(ya_ref, yb_ref, yc_ref, gl_ref, wb_ref, bg_ref, wo_ref, x_ref, g_ref, b_ref, o_ref):
    merged = None
    for n, y_ref in enumerate((ya_ref, yb_ref, yc_ref)):
        cols = slice(n * D_MODEL, (n + 1) * D_MODEL)
        branch = jnp.dot(y_ref[...], wb_ref[n], preferred_element_type=F32)
        term = _sigmoid(gl_ref[:, cols] + bg_ref[:, cols]) * branch
        merged = term if merged is None else merged + term
    mix = jnp.dot(merged.astype(BF16), wo_ref[...], preferred_element_type=F32)
    o_ref[...] = _layer_norm(ALPHA * x_ref[...] + mix, g_ref[...], b_ref[...])


def _merge(ya, yb, yc, gl, w_branch, b_gate, w_out, x, g, b):
    n = x.shape[0]
    tm = ROW_TILE
    ytile = pl.BlockSpec((tm, BRANCH_WIDTH), lambda i: (i, 0))
    row = pl.BlockSpec((tm, D_MODEL), lambda i: (i, 0))
    vec = pl.BlockSpec((1, D_MODEL), lambda i: (0, 0))
    return pl.pallas_call(
        _merge_kernel,
        grid=(n // tm,),
        in_specs=[ytile, ytile, ytile,
                  pl.BlockSpec((tm, 3 * D_MODEL), lambda i: (i, 0)),
                  pl.BlockSpec((3, BRANCH_WIDTH, D_MODEL), lambda i: (0, 0, 0)),
                  pl.BlockSpec((1, 3 * D_MODEL), lambda i: (0, 0)),
                  pl.BlockSpec((D_MODEL, D_MODEL), lambda i: (0, 0)),
                  row, vec, vec],
        out_specs=row,
        out_shape=jax.ShapeDtypeStruct((n, D_MODEL), F32),
        compiler_params=_cparams("parallel"),
        name="merge",
    )(ya, yb, yc, gl, w_branch.astype(BF16), b_gate.reshape(1, -1), w_out.astype(BF16), x,
      g.reshape(1, -1), b.reshape(1, -1))


def _mlp_kernel(x_ref, wu_ref, wd_ref, g_ref, b_ref, o_ref, ob_ref, xb_ref, acc_ref):
    j = pl.program_id(1)

    @pl.when(j == 0)
    def _():
        xb_ref[...] = x_ref[...].astype(BF16)
        acc_ref[...] = jnp.zeros_like(acc_ref)

    hid = jnp.maximum(jnp.dot(xb_ref[...], wu_ref[...], preferred_element_type=F32), 0.0)
    acc_ref[...] += jnp.dot((hid * hid).astype(BF16), wd_ref[...], preferred_element_type=F32)

    @pl.when(j == pl.num_programs(1) - 1)
    def _():
        y = _layer_norm(ALPHA * x_ref[...] + acc_ref[...], g_ref[...], b_ref[...])
        o_ref[...] = y
        ob_ref[...] = y.astype(BF16)


def _mlp(x, w_up, w_down, g, b):
    n = x.shape[0]
    tm = ROW_TILE
    tf = D_MODEL
    row = pl.BlockSpec((tm, D_MODEL), lambda i, j: (i, 0))
    vec = pl.BlockSpec((1, D_MODEL), lambda i, j: (0, 0))
    return pl.pallas_call(
        _mlp_kernel,
        grid=(n // tm, D_FF // tf),
        in_specs=[row,
                  pl.BlockSpec((D_MODEL, tf), lambda i, j: (0, j)),
                  pl.BlockSpec((tf, D_MODEL), lambda i, j: (j, 0)),
                  vec, vec],
        out_specs=[row, row],
        out_shape=[jax.ShapeDtypeStruct((n, D_MODEL), F32), jax.ShapeDtypeStruct((n, D_MODEL), BF16)],
        scratch_shapes=[pltpu.VMEM((tm, D_MODEL), BF16), pltpu.VMEM((tm, D_MODEL), F32)],
        compiler_params=_cparams("parallel", "arbitrary"),
        name="mlp",
    )(x, w_up.astype(BF16), w_down.astype(BF16), g.reshape(1, -1), b.reshape(1, -1))


def kernel(x, ln0_g, ln0_b, w_in, pool_w, pool_scale, da_lambda, da_subln_g, rel_bias, rw_mu_prev, rw_mu_next, rw_w0, rw_w2, rw_a0, rw_a2, rw_g2, rw_k_k, rw_k_a, rw_r_k, rw_gn_g, rw_gn_b, w_branch, b_gate, w_out, ln1_g, ln1_b, w_up, w_down, ln2_g, ln2_b):
    b, s, dm = x.shape
    n = b * s
    assert dm == D_MODEL and n % ROW_TILE == 0 and s % ATTN_Q_TILE == 0 and s % ROW_TILE == 0
    c_uqkv = 4 * BRANCH_WIDTH
    c_zr = c_uqkv + RW_SHIFT_WIDTH

    strip = _bias_strip(rel_bias, s)
    xf, xb = _ln0(x.reshape(n, dm), ln0_g, ln0_b)
    for l in range(DEPTH):
        wl = w_in[l].astype(BF16)
        uqkv = _proj(xb, wl[:, :c_uqkv], 1024, "proj_uqkv").reshape(b, s, c_uqkv)
        zr = _proj(xb, wl[:, c_uqkv:c_zr], RW_SHIFT_WIDTH, "proj_zr").reshape(b, s, RW_SHIFT_WIDTH)
        gl = _proj(xb, wl[:, c_zr:], 1024, "proj_gate")

        y_a = _pool(uqkv, pool_w[l], pool_scale[l])
        lam_init = 0.8 - 0.6 * math.exp(-0.3 * l)
        y_b = _diff_attention(uqkv, strip, da_lambda[l], da_subln_g[l], lam_init)
        r, v, kk, lw, kd, be, gate, bonus = _rw_prep(
            zr, rw_mu_prev[l], rw_mu_next[l], rw_w0[l], rw_w2[l], rw_a0[l], rw_a2[l], rw_g2[l],
            rw_k_k[l], rw_k_a[l], rw_r_k[l])
        yf, yb = _rw_scan(r, v, kk, lw, kd, be)
        y_c = _rw_out(yf, yb, gate, bonus, rw_gn_g[l], rw_gn_b[l])

        xf = _merge(y_a.reshape(n, -1), y_b.reshape(n, -1), y_c.reshape(n, -1), gl,
                    w_branch[l], b_gate[l], w_out[l], xf, ln1_g[l], ln1_b[l])
        xf, xb = _mlp(xf, w_up[l], w_down[l], ln2_g[l], ln2_b[l])
    return xf.reshape(b, s, dm)
```

```python
import functools
import math

import numpy as np
import jax
import jax.numpy as jnp
from jax import lax
from jax.experimental import pallas as pl
from jax.experimental.pallas import tpu as pltpu

D_MODEL = 1024
DEPTH = 4
BRANCH_WIDTH = 512
POOL_WINDOWS = (2, 4, 8, 16)
POOL_GROUP_DIM = 128
DA_HEAD_DIM = 64
DA_HEADS = 4
REL_BUCKETS = 32
REL_MAX_DIST = 128
RW_WIDTH = 512
RW_HEAD_DIM = 64
RW_HEADS = 8
DECAY_RANK = 64
ICLR_RANK = 64
GATE_RANK = 128
RW_SHIFT_WIDTH = 3 * RW_WIDTH + 2 * DECAY_RANK + 2 * ICLR_RANK + GATE_RANK
GN_EPS = 64e-5
D_FF = 4 * D_MODEL
LN_EPS = 1e-5
ALPHA = (2.0 * DEPTH) ** 0.25
LOG2E = math.log2(math.e)

VMEM_LIMIT_BYTES = 56 * 1024 * 1024

ROW_TILE = 512
MLP_ROW_TILE = 1024
ATTN_Q_TILE = 256
RW_PREP_TILE = 256
RW_CHUNK = 64
RW_BATCH_BLOCK = 2
RW_INV_BASE = 8
INV_PASSES = 1
MIX_PASSES = 1

F32 = jnp.float32
BF16 = jnp.bfloat16
NN = (((1,), (0,)), ((), ()))
NT = (((1,), (1,)), ((), ()))
TN = (((0,), (0,)), ((), ()))


def _cparams(*sem):
    return pltpu.CompilerParams(dimension_semantics=sem, vmem_limit_bytes=VMEM_LIMIT_BYTES)


def _layer_norm(x, g, b):
    mu = jnp.mean(x, axis=-1, keepdims=True)
    xc = x - mu
    var = jnp.mean(xc * xc, axis=-1, keepdims=True)
    return xc * lax.rsqrt(var + LN_EPS) * g + b


def _split_bf16(x, terms):
    parts = []
    for _ in range(terms):
        p = x.astype(BF16)
        parts.append(p)
        x = x - p.astype(F32)
    return parts


def _mm_hilo(a, b_hi, b_lo):
    ah, al = _split_bf16(a, 2)
    return (jnp.dot(ah, b_hi, preferred_element_type=F32) + jnp.dot(ah, b_lo, preferred_element_type=F32)
            + jnp.dot(al, b_hi, preferred_element_type=F32))


def _mm(a, b, dims, passes):
    if passes == 1:
        return lax.dot_general(a.astype(BF16), b.astype(BF16), dims, preferred_element_type=F32)
    assert passes == 3 and dims == NN
    return _mm_hilo(a, *_split_bf16(b, 2))


def _ln0_kernel(x_ref, g_ref, b_ref, o_ref, ob_ref):
    y = _layer_norm(x_ref[...], g_ref[...], b_ref[...])
    o_ref[...] = y
    ob_ref[...] = y.astype(BF16)


def _ln0(x2, g, b):
    n = x2.shape[0]
    row = pl.BlockSpec((ROW_TILE, D_MODEL), lambda i: (i, 0))
    vec = pl.BlockSpec((1, D_MODEL), lambda i: (0, 0))
    return pl.pallas_call(
        _ln0_kernel,
        grid=(n // ROW_TILE,),
        in_specs=[row, vec, vec],
        out_specs=[row, row],
        out_shape=[jax.ShapeDtypeStruct((n, D_MODEL), F32), jax.ShapeDtypeStruct((n, D_MODEL), BF16)],
        compiler_params=_cparams("parallel"),
        name="ln0",
    )(x2, g.reshape(1, -1), b.reshape(1, -1))


def _proj_kernel(x_ref, w_ref, o_ref):
    o_ref[...] = jnp.dot(x_ref[...], w_ref[...], preferred_element_type=F32).astype(o_ref.dtype)


def _proj(xb, w, tn, name, out_dtype=F32):
    n, k = xb.shape
    m = w.shape[1]
    return pl.pallas_call(
        _proj_kernel,
        grid=(n // ROW_TILE, m // tn),
        in_specs=[pl.BlockSpec((ROW_TILE, k), lambda i, j: (i, 0)),
                  pl.BlockSpec((k, tn), lambda i, j: (0, j))],
        out_specs=pl.BlockSpec((ROW_TILE, tn), lambda i, j: (i, j)),
        out_shape=jax.ShapeDtypeStruct((n, m), out_dtype),
        compiler_params=_cparams("parallel", "arbitrary"),
        name=name,
    )(xb, w)


def _pool_kernel(u_ref, w_ref, s_ref, o_ref):
    s = u_ref.shape[1]
    t = lax.broadcasted_iota(jnp.int32, (s, POOL_GROUP_DIM), 0)
    for gi, win in enumerate(POOL_WINDOWS):
        left = win // 2
        right = win - 1 - left
        cols = slice(gi * POOL_GROUP_DIM, (gi + 1) * POOL_GROUP_DIM)
        x = u_ref[0, :, cols]
        acc = x
        for d in range(-left, right + 1):
            if d == 0:
                continue
            shifted = pltpu.roll(x, (-d) % s, axis=0)
            valid = jnp.logical_and(t + d >= 0, t + d < s)
            acc = acc + jnp.where(valid, shifted, 0.0)
        cnt = (jnp.minimum(t + right + 1, s) - jnp.maximum(t - left, 0)).astype(F32)
        mixed = acc / cnt - x
        y = jnp.dot(mixed.astype(BF16), w_ref[gi], preferred_element_type=F32)
        o_ref[0, :, cols] = (y * s_ref[:, cols]).astype(BF16)


def _pool(uq, pool_w, pool_scale):
    b, s, _ = uq.shape
    return pl.pallas_call(
        _pool_kernel,
        grid=(b,),
        in_specs=[pl.BlockSpec((1, s, BRANCH_WIDTH), lambda i: (i, 0, 0)),
                  pl.BlockSpec((len(POOL_WINDOWS), POOL_GROUP_DIM, POOL_GROUP_DIM), lambda i: (0, 0, 0)),
                  pl.BlockSpec((1, BRANCH_WIDTH), lambda i: (0, 0))],
        out_specs=pl.BlockSpec((1, s, BRANCH_WIDTH), lambda i: (i, 0, 0)),
        out_shape=jax.ShapeDtypeStruct((b, s, BRANCH_WIDTH), BF16),
        compiler_params=_cparams("parallel"),
        name="pool",
    )(uq, pool_w.astype(BF16), pool_scale.reshape(1, -1))


def _rel_bucket_np(rel):
    half = REL_BUCKETS // 2
    max_exact = half // 2
    n = np.abs(rel)
    nf = np.maximum(n, 1).astype(np.float64)
    large = max_exact + (np.log(nf / max_exact) / math.log(REL_MAX_DIST / max_exact)
                         * (half - max_exact)).astype(np.int32)
    large = np.minimum(large, half - 1)
    return (np.where(rel > 0, half, 0) + np.where(n < max_exact, n, large)).astype(np.int32)


def _bias_strip_kernel(tab_ref, bk_ref, o_ref):
    h = pl.program_id(0)
    bk = bk_ref[...]
    acc = jnp.zeros(bk.shape, F32)
    for j in range(REL_BUCKETS):
        acc = jnp.where(bk == j, tab_ref[j * DA_HEADS + h] * LOG2E, acc)
    o_ref[0] = acc


def _bias_strip(rel_bias, s):
    tq = ATTN_Q_TILE
    w = 2 * s - tq
    rel = np.arange(w)[None, :] - np.arange(tq)[:, None] - (s - tq)
    bucket = jnp.asarray(_rel_bucket_np(rel))
    return pl.pallas_call(
        _bias_strip_kernel,
        grid=(DA_HEADS,),
        in_specs=[pl.BlockSpec(memory_space=pltpu.SMEM),
                  pl.BlockSpec((tq, w), lambda h: (0, 0))],
        out_specs=pl.BlockSpec((1, tq, w), lambda h: (h, 0, 0)),
        out_shape=jax.ShapeDtypeStruct((DA_HEADS, tq, w), F32),
        compiler_params=_cparams("arbitrary"),
        name="bias_strip",
    )(rel_bias.reshape(-1), bucket)


def _attn_kernel(lam0_ref, lamv_ref, q_ref, k_ref, v_ref, strip_ref, g_ref, o_ref):
    tq = q_ref.shape[1]
    s = k_ref.shape[1]
    qi = pl.program_id(2)
    nq = pl.num_programs(2)
    lam_init = lam0_ref[0]
    lv = lamv_ref[...]
    lam = (jnp.exp(jnp.sum(lv[0:1] * lv[1:2], axis=-1, keepdims=True))
           - jnp.exp(jnp.sum(lv[2:3] * lv[3:4], axis=-1, keepdims=True)) + lam_init)

    q = q_ref[0] * (DA_HEAD_DIM ** -0.5 * LOG2E)
    lane = lax.broadcasted_iota(jnp.int32, q.shape, 1)
    k = k_ref[0]
    v = v_ref[0]
    off = pl.multiple_of((nq - 1 - qi) * tq, tq)
    bias = strip_ref[0, :, pl.ds(off, s)]

    def softmax_times_v(qm):
        logits = lax.dot_general(qm.astype(BF16), k, NT, preferred_element_type=F32) + bias
        m = jnp.max(logits, axis=-1, keepdims=True)
        e = jnp.exp2(logits - m)
        l = jnp.sum(e, axis=-1, keepdims=True)
        return jnp.dot(e.astype(BF16), v, preferred_element_type=F32) / l

    o = (softmax_times_v(jnp.where(lane < DA_HEAD_DIM, q, 0.0))
         - lam * softmax_times_v(jnp.where(lane >= DA_HEAD_DIM, q, 0.0)))
    o = o * lax.rsqrt(jnp.mean(o * o, axis=-1, keepdims=True) + 1e-5) * g_ref[...]
    o_ref[0] = (o * (1.0 - lam_init)).astype(BF16)


def _diff_attention(uq, kv, strip, da_lambda, subln_g, lam_init):
    b, s, _ = uq.shape
    tq = ATTN_Q_TILE
    hw = 2 * DA_HEAD_DIM
    q_blk0 = BRANCH_WIDTH // hw
    return pl.pallas_call(
        _attn_kernel,
        grid=(b, DA_HEADS, s // tq),
        in_specs=[pl.BlockSpec(memory_space=pltpu.SMEM),
                  pl.BlockSpec((4, DA_HEAD_DIM), lambda bi, h, qi: (0, 0)),
                  pl.BlockSpec((1, tq, hw), lambda bi, h, qi: (bi, qi, q_blk0 + h)),
                  pl.BlockSpec((1, s, hw), lambda bi, h, qi: (bi, 0, h)),
                  pl.BlockSpec((1, s, hw), lambda bi, h, qi: (bi, 0, DA_HEADS + h)),
                  pl.BlockSpec((1, tq, strip.shape[2]), lambda bi, h, qi: (h, 0, 0)),
                  pl.BlockSpec((1, hw), lambda bi, h, qi: (0, 0))],
        out_specs=pl.BlockSpec((1, tq, hw), lambda bi, h, qi: (bi, qi, h)),
        out_shape=jax.ShapeDtypeStruct((b, s, BRANCH_WIDTH), BF16),
        compiler_params=_cparams("parallel", "parallel", "arbitrary"),
        name="diff_attn",
    )(jnp.full((1,), lam_init, F32), da_lambda, uq, kv, kv, strip, subln_g.reshape(1, -1))


def _head_segment_matrix():
    return jnp.asarray(np.kron(np.eye(RW_HEADS), np.ones((RW_HEAD_DIM, RW_HEAD_DIM))), dtype=BF16)


def _seg_sum(x, seg):
    return sum(jnp.dot(part, seg, preferred_element_type=F32) for part in _split_bf16(x, 2))


def _sigmoid(x):
    return 1.0 / (1.0 + jnp.exp(-x))


def _rw_prep_kernel(z_ref, zp_ref, zn_ref, mup_ref, mun_ref, w0_ref, w2h_ref, w2l_ref, a0_ref, a2h_ref, a2l_ref,
                    g2h_ref, g2l_ref,
                    kk_w_ref, ka_w_ref, rk_w_ref, seg_ref,
                    r_ref, v_ref, kk_ref, lw_ref, kd_ref, be_ref, gate_ref, bonus_ref):
    i = pl.program_id(1)
    n = pl.num_programs(1)
    z = z_ref[0]
    ts = z.shape[0]
    slab = 8
    t = lax.broadcasted_iota(jnp.int32, (slab, 1), 0)
    prev_row = jnp.where(i > 0, zp_ref[0, slab - 1:slab, :], 0.0)
    next_row = jnp.where(i < n - 1, zn_ref[0, 0:1, :], 0.0)
    prev = pltpu.roll(z, 1, axis=0)
    prev = jnp.concatenate([jnp.where(t == 0, prev_row, prev[:slab]), prev[slab:]], axis=0)
    nxt = pltpu.roll(z, ts - 1, axis=0)
    nxt = jnp.concatenate([nxt[:ts - slab], jnp.where(t == slab - 1, next_row, nxt[ts - slab:])], axis=0)
    z = z + mup_ref[...] * (prev - z) + mun_ref[...] * (nxt - z)

    w = RW_WIDTH
    r = z[:, 0:w]
    k = z[:, w:2 * w]
    v = z[:, 2 * w:3 * w]
    lw = z[:, 3 * w:3 * w + 2 * DECAY_RANK]
    la = z[:, 3 * w + 2 * DECAY_RANK:3 * w + 2 * DECAY_RANK + 2 * ICLR_RANK]
    lg = z[:, 3 * w + 2 * DECAY_RANK + 2 * ICLR_RANK:]

    x = w0_ref[...] + _mm_hilo(jnp.tanh(lw), w2h_ref[...], w2l_ref[...])
    log_decay = -math.exp(-0.5) / (1.0 + jnp.exp(-x))
    iclr = _sigmoid(a0_ref[...] + _mm_hilo(la, a2h_ref[...], a2l_ref[...]))
    gate = _mm_hilo(_sigmoid(lg), g2h_ref[...], g2l_ref[...])

    seg = seg_ref[...]
    kk = k * kk_w_ref[...]
    kk = kk * lax.rsqrt(jnp.maximum(_seg_sum(kk * kk, seg), 1e-24))
    bonus = _seg_sum(r * k * rk_w_ref[...], seg) * v

    r_ref[0] = r
    v_ref[0] = v
    kk_ref[0] = kk
    lw_ref[0] = log_decay
    gate_ref[0] = gate
    bonus_ref[0] = bonus
    ka = ka_w_ref[...]
    for d in range(2):
        a_d = iclr[:, d * w:(d + 1) * w]
        kd_ref[0, :, d * w:(d + 1) * w] = k * (1.0 + (a_d - 1.0) * ka)
        be_ref[0, :, d * w:(d + 1) * w] = kk * a_d


def _block_diag2(m):
    z = jnp.zeros_like(m[0])
    return jnp.concatenate([jnp.concatenate([m[0], z], axis=1), jnp.concatenate([z, m[1]], axis=1)], axis=0)


def _rw_prep(zr, mu_prev, mu_next, w0, w2, a0, a2, g2, k_k, k_a, r_k):
    b, s, c = zr.shape
    ts = RW_PREP_TILE
    w = RW_WIDTH
    seg = _head_segment_matrix()
    nblk8 = s // 8

    def vec(width):
        return pl.BlockSpec((1, width), lambda bi, i: (0, 0))

    def mat(rows, cols):
        return pl.BlockSpec((rows, cols), lambda bi, i: (0, 0))

    def out(width):
        return pl.BlockSpec((1, ts, width), lambda bi, i: (bi, i, 0))

    shapes = [jax.ShapeDtypeStruct((b, s, width), F32) for width in (w, w, w, 2 * w, 2 * w, 2 * w, w, w)]
    return pl.pallas_call(
        _rw_prep_kernel,
        grid=(b, s // ts),
        in_specs=[pl.BlockSpec((1, ts, c), lambda bi, i: (bi, i, 0)),
                  pl.BlockSpec((1, 8, c), lambda bi, i: (bi, jnp.maximum(i * (ts // 8) - 1, 0), 0)),
                  pl.BlockSpec((1, 8, c), lambda bi, i: (bi, jnp.minimum((i + 1) * (ts // 8), nblk8 - 1), 0)),
                  vec(c), vec(c), vec(2 * w), mat(2 * DECAY_RANK, 2 * w), mat(2 * DECAY_RANK, 2 * w),
                  vec(2 * w), mat(2 * ICLR_RANK, 2 * w), mat(2 * ICLR_RANK, 2 * w),
                  mat(GATE_RANK, w), mat(GATE_RANK, w), vec(w), vec(w), vec(w), mat(w, w)],
        out_specs=[out(w), out(w), out(w), out(2 * w), out(2 * w), out(2 * w), out(w), out(w)],
        out_shape=shapes,
        compiler_params=_cparams("parallel", "arbitrary"),
        name="rw_prep",
    )(zr, zr, zr, mu_prev.reshape(1, -1), mu_next.reshape(1, -1), w0.reshape(1, -1), *_split_bf16(_block_diag2(w2), 2),
      a0.reshape(1, -1), *_split_bf16(_block_diag2(a2), 2), *_split_bf16(g2, 2),
      k_k.reshape(1, -1), k_a.reshape(1, -1), r_k.reshape(1, -1), seg)


def _unit_triangular_inverses(mats):
    n = mats[0].shape[0]
    ti = lax.broadcasted_iota(jnp.int32, (n, n), 0)
    si = lax.broadcasted_iota(jnp.int32, (n, n), 1)

    def same_block(m):
        return (ti // m) == (si // m)

    mm = functools.partial(_mm, dims=NN, passes=INV_PASSES)
    eye = (ti == si).astype(F32)
    base = same_block(RW_INV_BASE)
    diag = [jnp.where(base, a, 0.0) for a in mats]
    power = [mm(a, a) for a in diag]
    inv = [mm(eye + a, eye + p) for a, p in zip(diag, power)]
    m = 4
    while m < RW_INV_BASE:
        power = [mm(p, p) for p in power]
        inv = [mm(t, eye + p) for t, p in zip(inv, power)]
        m *= 2
    m = RW_INV_BASE
    while m < n:
        ring = jnp.logical_and(same_block(2 * m), jnp.logical_not(same_block(m)))
        tmp = [mm(jnp.where(ring, a, 0.0), t) for a, t in zip(mats, inv)]
        inv = [t + mm(t, x) for t, x in zip(inv, tmp)]
        m *= 2
    return inv


def _rw_scan_kernel(rf_ref, vf_ref, kkf_ref, lwf_ref, kdf_ref, bef_ref,
                    rb_ref, vb_ref, kkb_ref, lwb_ref, kdb_ref, beb_ref, yf_ref, yb_ref, state_ref):
    @pl.when(pl.program_id(1) == 0)
    def _():
        state_ref[...] = jnp.zeros_like(state_ref)

    n = RW_CHUNK
    hd = RW_HEAD_DIM
    ti = lax.broadcasted_iota(jnp.int32, (2 * n, 2 * n), 0)
    si = lax.broadcasted_iota(jnp.int32, (2 * n, 2 * n), 1)
    zeros = jnp.zeros((n, hd), F32)

    chains = []
    for d, (r_ref, v_ref, kk_ref, lw_ref, kd_ref, be_ref) in enumerate((
            (rf_ref, vf_ref, kkf_ref, lwf_ref, kdf_ref, bef_ref),
            (rb_ref, vb_ref, kkb_ref, lwb_ref, kdb_ref, beb_ref))):
        lag = (ti % n - si % n) * (1 - 2 * d)
        keep = lag >= jnp.where(ti < n, 1, 0)
        tri = (lag[:n, :n] >= 0).astype(BF16)
        for bb in range(RW_BATCH_BLOCK):
            lw = lw_ref[bb]
            cum = sum(jnp.dot(tri, part, preferred_element_type=F32) for part in _split_bf16(lw, 3))
            e_neg = jnp.exp(-cum)
            ar_all = jnp.concatenate([-kk_ref[bb] * jnp.exp(cum - lw), r_ref[bb] * jnp.exp(cum)], axis=0)
            bk_all = jnp.concatenate([be_ref[bb] * e_neg, kd_ref[bb] * e_neg], axis=0)
            v_all = v_ref[bb]
            e_tot = jnp.exp(jnp.sum(lw, axis=0, keepdims=True))
            for h in range(RW_HEADS):
                cols = slice(h * hd, (h + 1) * hd)
                chains.append(dict(bb=bb, d=d, h=h, cols=cols, keep=keep, ar=ar_all[:, cols], bk=bk_all[:, cols],
                                   v=v_all[:, cols], decay=e_tot[:, cols], st=state_ref[bb, d, h]))

    for ch in chains:
        ch["m"] = jnp.where(ch["keep"], _mm(ch["ar"], ch["bk"], NT, MIX_PASSES), 0.0)
    for ch in chains:
        ch["xy"] = (_mm(ch["ar"], ch["st"], NT, MIX_PASSES)
                    + _mm(ch["m"], jnp.concatenate([zeros, ch["v"]], axis=0), NN, MIX_PASSES))
    invs = _unit_triangular_inverses([ch["m"][:n, :n] for ch in chains])
    for ch, inv in zip(chains, invs):
        ch["u"] = _mm(inv, ch["xy"][:n], NN, INV_PASSES)
    for ch in chains:
        y = ch["xy"][n:] + _mm(ch["m"][n:], jnp.concatenate([ch["u"], zeros], axis=0), NN, MIX_PASSES)
        (yf_ref, yb_ref)[ch["d"]][ch["bb"], :, ch["cols"]] = y
    for ch in chains:
        upd = _mm(jnp.concatenate([ch["u"], ch["v"]], axis=0), ch["bk"], TN, MIX_PASSES)
        state_ref[ch["bb"], ch["d"], ch["h"]] = (ch["st"] + upd) * ch["decay"]


def _rw_scan(r, v, kk, lw, kd, be):
    b, s, w = r.shape
    n = RW_CHUNK
    nc = s // n
    nb = RW_BATCH_BLOCK

    def spec(d, col):
        if d == 0:
            return pl.BlockSpec((nb, n, w), lambda bi, c: (bi, c, col))
        return pl.BlockSpec((nb, n, w), lambda bi, c: (bi, nc - 1 - c, col))

    in_specs = []
    for d in range(2):
        in_specs += [spec(d, 0), spec(d, 0), spec(d, 0), spec(d, d), spec(d, d), spec(d, d)]
    return pl.pallas_call(
        _rw_scan_kernel,
        grid=(b // nb, nc),
        in_specs=in_specs,
        out_specs=[spec(0, 0), spec(1, 0)],
        out_shape=[jax.ShapeDtypeStruct((b, s, w), F32)] * 2,
        scratch_shapes=[pltpu.VMEM((nb, 2, RW_HEADS, RW_HEAD_DIM, RW_HEAD_DIM), F32)],
        compiler_params=_cparams("parallel", "arbitrary"),
        name="rw_scan",
    )(r, v, kk, lw, kd, be, r, v, kk, lw, kd, be)


def _rw_out_kernel(yf_ref, yb_ref, gate_ref, bonus_ref, g_ref, b_ref, seg_ref, o_ref):
    y = yf_ref[0] + yb_ref[0]
    seg = seg_ref[...]
    mu = _seg_sum(y, seg) * (1.0 / RW_HEAD_DIM)
    yc = y - mu
    var = _seg_sum(yc * yc, seg) * (1.0 / RW_HEAD_DIM)
    yn = yc * lax.rsqrt(var + GN_EPS) * g_ref[...] + b_ref[...]
    o_ref[0] = ((yn + bonus_ref[0]) * gate_ref[0]).astype(BF16)


def _rw_out(yf, yb, gate, bonus, gn_g, gn_b):
    b, s, w = yf.shape
    ts = ROW_TILE
    seg = _head_segment_matrix()
    tile = pl.BlockSpec((1, ts, w), lambda bi, i: (bi, i, 0))
    vec = pl.BlockSpec((1, w), lambda bi, i: (0, 0))
    return pl.pallas_call(
        _rw_out_kernel,
        grid=(b, s // ts),
        in_specs=[tile, tile, tile, tile, vec, vec,
                  pl.BlockSpec((w, w), lambda bi, i: (0, 0))],
        out_specs=tile,
        out_shape=jax.ShapeDtypeStruct((b, s, w), BF16),
        compiler_params=_cparams("parallel", "parallel"),
        name="rw_out",
    )(yf, yb, gate, bonus, gn_g.reshape(1, -1), gn_b.reshape(1, -1), seg)


def _merge_kernel(ya_ref, yb_ref, yc_ref, gl_ref, wb_ref, bg_ref, wo_ref, x_ref, g_ref, b_ref, o_ref):
    merged = None
    for n, y_ref in enumerate((ya_ref, yb_ref, yc_ref)):
        cols = slice(n * D_MODEL, (n + 1) * D_MODEL)
        branch = jnp.dot(y_ref[...], wb_ref[n], preferred_element_type=F32)
        term = _sigmoid(gl_ref[:, cols] + bg_ref[:, cols]) * branch
        merged = term if merged is None else merged + term
    mix = jnp.dot(merged.astype(BF16), wo_ref[...], preferred_element_type=F32)
    o_ref[...] = _layer_norm(ALPHA * x_ref[...] + mix, g_ref[...], b_ref[...])


def _merge(ya, yb, yc, gl, w_branch, b_gate, w_out, x, g, b):
    n = x.shape[0]
    tm = ROW_TILE
    ytile = pl.BlockSpec((tm, BRANCH_WIDTH), lambda i: (i, 0))
    row = pl.BlockSpec((tm, D_MODEL), lambda i: (i, 0))
    vec = pl.BlockSpec((1, D_MODEL), lambda i: (0, 0))
    return pl.pallas_call(
        _merge_kernel,
        grid=(n // tm,),
        in_specs=[ytile, ytile, ytile,
                  pl.BlockSpec((tm, 3 * D_MODEL), lambda i: (i, 0)),
                  pl.BlockSpec((3, BRANCH_WIDTH, D_MODEL), lambda i: (0, 0, 0)),
                  pl.BlockSpec((1, 3 * D_MODEL), lambda i: (0, 0)),
                  pl.BlockSpec((D_MODEL, D_MODEL), lambda i: (0, 0)),
                  row, vec, vec],
        out_specs=row,
        out_shape=jax.ShapeDtypeStruct((n, D_MODEL), F32),
        compiler_params=_cparams("parallel"),
        name="merge",
    )(ya, yb, yc, gl, w_branch.astype(BF16), b_gate.reshape(1, -1), w_out.astype(BF16), x,
      g.reshape(1, -1), b.reshape(1, -1))


def _mlp_kernel(x_ref, wu_ref, wd_ref, g_ref, b_ref, o_ref, ob_ref, xb_ref, acc_ref):
    j = pl.program_id(1)

    @pl.when(j == 0)
    def _():
        xb_ref[...] = x_ref[...].astype(BF16)
        acc_ref[...] = jnp.zeros_like(acc_ref)

    hid = jnp.maximum(jnp.dot(xb_ref[...], wu_ref[...], preferred_element_type=F32), 0.0)
    acc_ref[...] += jnp.dot((hid * hid).astype(BF16), wd_ref[...], preferred_element_type=F32)

    @pl.when(j == pl.num_programs(1) - 1)
    def _():
        y = _layer_norm(ALPHA * x_ref[...] + acc_ref[...], g_ref[...], b_ref[...])
        o_ref[...] = y
        ob_ref[...] = y.astype(BF16)


def _mlp(x, w_up, w_down, g, b):
    n = x.shape[0]
    tm = MLP_ROW_TILE
    tf = D_MODEL
    row = pl.BlockSpec((tm, D_MODEL), lambda i, j: (i, 0))
    vec = pl.BlockSpec((1, D_MODEL), lambda i, j: (0, 0))
    return pl.pallas_call(
        _mlp_kernel,
        grid=(n // tm, D_FF // tf),
        in_specs=[row,
                  pl.BlockSpec((D_MODEL, tf), lambda i, j: (0, j)),
                  pl.BlockSpec((tf, D_MODEL), lambda i, j: (j, 0)),
                  vec, vec],
        out_specs=[row, row],
        out_shape=[jax.ShapeDtypeStruct((n, D_MODEL), F32), jax.ShapeDtypeStruct((n, D_MODEL), BF16)],
        scratch_shapes=[pltpu.VMEM((tm, D_MODEL), BF16), pltpu.VMEM((tm, D_MODEL), F32)],
        compiler_params=_cparams("parallel", "arbitrary"),
        name="mlp",
    )(x, w_up.astype(BF16), w_down.astype(BF16), g.reshape(1, -1), b.reshape(1, -1))


def kernel(x, ln0_g, ln0_b, w_in, pool_w, pool_scale, da_lambda, da_subln_g, rel_bias, rw_mu_prev, rw_mu_next, rw_w0, rw_w2, rw_a0, rw_a2, rw_g2, rw_k_k, rw_k_a, rw_r_k, rw_gn_g, rw_gn_b, w_branch, b_gate, w_out, ln1_g, ln1_b, w_up, w_down, ln2_g, ln2_b):
    b, s, dm = x.shape
    n = b * s
    assert dm == D_MODEL and n % MLP_ROW_TILE == 0 and n % ROW_TILE == 0
    assert s % ATTN_Q_TILE == 0 and s % ROW_TILE == 0 and s % RW_PREP_TILE == 0
    assert b % RW_BATCH_BLOCK == 0 and s % RW_CHUNK == 0
    c_uq = 2 * BRANCH_WIDTH
    c_uqkv = 4 * BRANCH_WIDTH
    c_zr = c_uqkv + RW_SHIFT_WIDTH

    strip = _bias_strip(rel_bias, s)
    xf, xb = _ln0(x.reshape(n, dm), ln0_g, ln0_b)
    for l in range(DEPTH):
        wl = w_in[l].astype(BF16)
        uq = _proj(xb, wl[:, :c_uq], c_uq, "proj_uq").reshape(b, s, c_uq)
        kv = _proj(xb, wl[:, c_uq:c_uqkv], c_uq, "proj_kv", BF16).reshape(b, s, c_uq)
        zr = _proj(xb, wl[:, c_uqkv:c_zr], RW_SHIFT_WIDTH, "proj_zr").reshape(b, s, RW_SHIFT_WIDTH)
        gl = _proj(xb, wl[:, c_zr:], 1024, "proj_gate")

        y_a = _pool(uq, pool_w[l], pool_scale[l])
        lam_init = 0.8 - 0.6 * math.exp(-0.3 * l)
        y_b = _diff_attention(uq, kv, strip, da_lambda[l], da_subln_g[l], lam_init)
        r, v, kk, lw, kd, be, gate, bonus = _rw_prep(
            zr, rw_mu_prev[l], rw_mu_next[l], rw_w0[l], rw_w2[l], rw_a0[l], rw_a2[l], rw_g2[l],
            rw_k_k[l], rw_k_a[l], rw_r_k[l])
        yf, yb = _rw_scan(r, v, kk, lw, kd, be)
        y_c = _rw_out(yf, yb, gate, bonus, rw_gn_g[l], rw_gn_b[l])

        xf = _merge(y_a.reshape(n, -1), y_b.reshape(n, -1), y_c.reshape(n, -1), gl,
                    w_branch[l], b_gate[l], w_out[l], xf, ln1_g[l], ln1_b[l])
        xf, xb = _mlp(xf, w_up[l], w_down[l], ln2_g[l], ln2_b[l])
    return xf.reshape(b, s, dm)
```

```python
import functools
import math

import numpy as np
import jax
import jax.numpy as jnp
from jax import lax
from jax.experimental import pallas as pl
from jax.experimental.pallas import tpu as pltpu

D_MODEL = 1024
DEPTH = 4
BRANCH_WIDTH = 512
POOL_WINDOWS = (2, 4, 8, 16)
POOL_GROUP_DIM = 128
DA_HEAD_DIM = 64
DA_HEADS = 4
REL_BUCKETS = 32
REL_MAX_DIST = 128
RW_WIDTH = 512
RW_HEAD_DIM = 64
RW_HEADS = 8
DECAY_RANK = 64
ICLR_RANK = 64
GATE_RANK = 128
RW_SHIFT_WIDTH = 3 * RW_WIDTH + 2 * DECAY_RANK + 2 * ICLR_RANK + GATE_RANK
GN_EPS = 64e-5
D_FF = 4 * D_MODEL
LN_EPS = 1e-5
ALPHA = (2.0 * DEPTH) ** 0.25
LOG2E = math.log2(math.e)

VMEM_LIMIT_BYTES = 56 * 1024 * 1024

ROW_TILE = 512
MLP_ROW_TILE = 512
MLP_FF_SLICE = 1024
ATTN_Q_TILE = 256
ATTN_HEADS_PER_STEP = 2
RW_PREP_TILE = 256
RW_HALO = 16
RW_CHUNK = 64
RW_BATCH_BLOCK = 2
RW_INV_BASE = 8
INV_PASSES = 1
MIX_PASSES = 1

F32 = jnp.float32
BF16 = jnp.bfloat16
NN = (((1,), (0,)), ((), ()))
NT = (((1,), (1,)), ((), ()))
TN = (((0,), (0,)), ((), ()))


def _cparams(*sem):
    return pltpu.CompilerParams(dimension_semantics=sem, vmem_limit_bytes=VMEM_LIMIT_BYTES)


def _layer_norm(x, g, b):
    mu = jnp.mean(x, axis=-1, keepdims=True)
    xc = x - mu
    var = jnp.mean(xc * xc, axis=-1, keepdims=True)
    return xc * lax.rsqrt(var + LN_EPS) * g + b


def _split_bf16(x, terms):
    parts = []
    for _ in range(terms):
        p = x.astype(BF16)
        parts.append(p)
        x = x - p.astype(F32)
    return parts


def _mm_hilo(a, b_hi, b_lo):
    ah, al = _split_bf16(a, 2)
    return (jnp.dot(ah, b_hi, preferred_element_type=F32) + jnp.dot(ah, b_lo, preferred_element_type=F32)
            + jnp.dot(al, b_hi, preferred_element_type=F32))


def _mm(a, b, dims, passes):
    if passes == 1:
        return lax.dot_general(a.astype(BF16), b.astype(BF16), dims, preferred_element_type=F32)
    assert passes == 3 and dims == NN
    return _mm_hilo(a, *_split_bf16(b, 2))


def _ln0_kernel(x_ref, g_ref, b_ref, o_ref, ob_ref):
    y = _layer_norm(x_ref[...], g_ref[...], b_ref[...])
    o_ref[...] = y
    ob_ref[...] = y.astype(BF16)


def _ln0(x2, g, b):
    n = x2.shape[0]
    row = pl.BlockSpec((ROW_TILE, D_MODEL), lambda i: (i, 0))
    vec = pl.BlockSpec((1, D_MODEL), lambda i: (0, 0))
    return pl.pallas_call(
        _ln0_kernel,
        grid=(n // ROW_TILE,),
        in_specs=[row, vec, vec],
        out_specs=[row, row],
        out_shape=[jax.ShapeDtypeStruct((n, D_MODEL), F32), jax.ShapeDtypeStruct((n, D_MODEL), BF16)],
        compiler_params=_cparams("parallel"),
        name="ln0",
    )(x2, g.reshape(1, -1), b.reshape(1, -1))


def _proj_kernel(x_ref, w_ref, o_ref):
    o_ref[...] = jnp.dot(x_ref[...], w_ref[...], preferred_element_type=F32).astype(o_ref.dtype)


def _proj(xb, w, tn, name, out_dtype=F32):
    n, k = xb.shape
    m = w.shape[1]
    return pl.pallas_call(
        _proj_kernel,
        grid=(n // ROW_TILE, m // tn),
        in_specs=[pl.BlockSpec((ROW_TILE, k), lambda i, j: (i, 0)),
                  pl.BlockSpec((k, tn), lambda i, j: (0, j))],
        out_specs=pl.BlockSpec((ROW_TILE, tn), lambda i, j: (i, j)),
        out_shape=jax.ShapeDtypeStruct((n, m), out_dtype),
        compiler_params=_cparams("parallel", "arbitrary"),
        name=name,
    )(xb, w)


def _pool_kernel(u_ref, w_ref, s_ref, o_ref):
    s = u_ref.shape[1]
    t = lax.broadcasted_iota(jnp.int32, (s, POOL_GROUP_DIM), 0)
    for gi, win in enumerate(POOL_WINDOWS):
        left = win // 2
        right = win - 1 - left
        cols = slice(gi * POOL_GROUP_DIM, (gi + 1) * POOL_GROUP_DIM)
        x = u_ref[0, :, cols]
        acc = x
        for d in range(-left, right + 1):
            if d == 0:
                continue
            shifted = pltpu.roll(x, (-d) % s, axis=0)
            valid = jnp.logical_and(t + d >= 0, t + d < s)
            acc = acc + jnp.where(valid, shifted, 0.0)
        cnt = (jnp.minimum(t + right + 1, s) - jnp.maximum(t - left, 0)).astype(F32)
        mixed = acc / cnt - x
        y = jnp.dot(mixed.astype(BF16), w_ref[gi], preferred_element_type=F32)
        o_ref[0, :, cols] = (y * s_ref[:, cols]).astype(BF16)


def _pool(uq, pool_w, pool_scale):
    b, s, _ = uq.shape
    return pl.pallas_call(
        _pool_kernel,
        grid=(b,),
        in_specs=[pl.BlockSpec((1, s, BRANCH_WIDTH), lambda i: (i, 0, 0)),
                  pl.BlockSpec((len(POOL_WINDOWS), POOL_GROUP_DIM, POOL_GROUP_DIM), lambda i: (0, 0, 0)),
                  pl.BlockSpec((1, BRANCH_WIDTH), lambda i: (0, 0))],
        out_specs=pl.BlockSpec((1, s, BRANCH_WIDTH), lambda i: (i, 0, 0)),
        out_shape=jax.ShapeDtypeStruct((b, s, BRANCH_WIDTH), BF16),
        compiler_params=_cparams("parallel"),
        name="pool",
    )(uq, pool_w.astype(BF16), pool_scale.reshape(1, -1))


def _rel_bucket_np(rel):
    half = REL_BUCKETS // 2
    max_exact = half // 2
    n = np.abs(rel)
    nf = np.maximum(n, 1).astype(np.float64)
    large = max_exact + (np.log(nf / max_exact) / math.log(REL_MAX_DIST / max_exact)
                         * (half - max_exact)).astype(np.int32)
    large = np.minimum(large, half - 1)
    return (np.where(rel > 0, half, 0) + np.where(n < max_exact, n, large)).astype(np.int32)


def _bias_strip_kernel(tab_ref, bk_ref, o_ref):
    h = pl.program_id(0)
    bk = bk_ref[...]
    acc = jnp.zeros(bk.shape, F32)
    for j in range(REL_BUCKETS):
        acc = jnp.where(bk == j, tab_ref[j * DA_HEADS + h] * LOG2E, acc)
    o_ref[0] = acc


def _bias_strip(rel_bias, s):
    tq = ATTN_Q_TILE
    w = 2 * s - tq
    rel = np.arange(w)[None, :] - np.arange(tq)[:, None] - (s - tq)
    bucket = jnp.asarray(_rel_bucket_np(rel))
    return pl.pallas_call(
        _bias_strip_kernel,
        grid=(DA_HEADS,),
        in_specs=[pl.BlockSpec(memory_space=pltpu.SMEM),
                  pl.BlockSpec((tq, w), lambda h: (0, 0))],
        out_specs=pl.BlockSpec((1, tq, w), lambda h: (h, 0, 0)),
        out_shape=jax.ShapeDtypeStruct((DA_HEADS, tq, w), F32),
        compiler_params=_cparams("arbitrary"),
        name="bias_strip",
    )(rel_bias.reshape(-1), bucket)


def _attn_kernel(lam0_ref, lamv_ref, q_ref, k_ref, v_ref, strip_ref, g_ref, o_ref):
    tq = q_ref.shape[1]
    s = k_ref.shape[1]
    hw = 2 * DA_HEAD_DIM
    qi = pl.program_id(2)
    nq = pl.num_programs(2)
    lam_init = lam0_ref[0]
    lv = lamv_ref[...]
    lam = (jnp.exp(jnp.sum(lv[0:1] * lv[1:2], axis=-1, keepdims=True))
           - jnp.exp(jnp.sum(lv[2:3] * lv[3:4], axis=-1, keepdims=True)) + lam_init)
    off = pl.multiple_of((nq - 1 - qi) * tq, tq)
    lane = lax.broadcasted_iota(jnp.int32, (tq, hw), 1)

    def logits(h, first_half):
        cols = slice(h * hw, (h + 1) * hw)
        q = q_ref[0, :, cols] * (DA_HEAD_DIM ** -0.5 * LOG2E)
        qm = jnp.where((lane < DA_HEAD_DIM) == first_half, q, 0.0).astype(BF16)
        return (lax.dot_general(qm, k_ref[0, :, cols], NT, preferred_element_type=F32)
                + strip_ref[h, :, pl.ds(off, s)])

    def softmax_times_v(h, lg):
        m = jnp.max(lg, axis=-1, keepdims=True)
        e = jnp.exp2(lg - m)
        l = jnp.sum(e, axis=-1, keepdims=True)
        return jnp.dot(e.astype(BF16), v_ref[0, :, h * hw:(h + 1) * hw], preferred_element_type=F32) / l

    streams = [(h, first) for h in range(ATTN_HEADS_PER_STEP) for first in (True, False)]
    pending = logits(*streams[0])
    outs = []
    for idx, (h, _) in enumerate(streams):
        lg = pending
        if idx + 1 < len(streams):
            pending = logits(*streams[idx + 1])
        outs.append(softmax_times_v(h, lg))
    for h in range(ATTN_HEADS_PER_STEP):
        o = outs[2 * h] - lam * outs[2 * h + 1]
        o = o * lax.rsqrt(jnp.mean(o * o, axis=-1, keepdims=True) + 1e-5) * g_ref[...]
        o_ref[0, :, h * hw:(h + 1) * hw] = (o * (1.0 - lam_init)).astype(BF16)


def _diff_attention(uq, kv, strip, da_lambda, subln_g, lam_init):
    b, s, _ = uq.shape
    tq = ATTN_Q_TILE
    nh = ATTN_HEADS_PER_STEP
    hw = 2 * DA_HEAD_DIM
    bw = nh * hw
    q_blk0 = BRANCH_WIDTH // bw
    v_blk0 = BRANCH_WIDTH // bw
    return pl.pallas_call(
        _attn_kernel,
        grid=(b, DA_HEADS // nh, s // tq),
        in_specs=[pl.BlockSpec(memory_space=pltpu.SMEM),
                  pl.BlockSpec((4, DA_HEAD_DIM), lambda bi, hp, qi: (0, 0)),
                  pl.BlockSpec((1, tq, bw), lambda bi, hp, qi: (bi, qi, q_blk0 + hp)),
                  pl.BlockSpec((1, s, bw), lambda bi, hp, qi: (bi, 0, hp)),
                  pl.BlockSpec((1, s, bw), lambda bi, hp, qi: (bi, 0, v_blk0 + hp)),
                  pl.BlockSpec((nh, tq, strip.shape[2]), lambda bi, hp, qi: (hp, 0, 0)),
                  pl.BlockSpec((1, hw), lambda bi, hp, qi: (0, 0))],
        out_specs=pl.BlockSpec((1, tq, bw), lambda bi, hp, qi: (bi, qi, hp)),
        out_shape=jax.ShapeDtypeStruct((b, s, BRANCH_WIDTH), BF16),
        compiler_params=_cparams("parallel", "parallel", "arbitrary"),
        name="diff_attn",
    )(jnp.full((1,), lam_init, F32), da_lambda, uq, kv, kv, strip, subln_g.reshape(1, -1))


SEG_LANES = 256


def _head_segment_matrix():
    heads = SEG_LANES // RW_HEAD_DIM
    return jnp.asarray(np.kron(np.eye(heads), np.ones((RW_HEAD_DIM, RW_HEAD_DIM))), dtype=BF16)


def _seg_sum(x, seg):
    parts = _split_bf16(x, 2)
    blocks = []
    for c in range(x.shape[1] // SEG_LANES):
        cols = slice(c * SEG_LANES, (c + 1) * SEG_LANES)
        blocks.append(sum(jnp.dot(p[:, cols], seg, preferred_element_type=F32) for p in parts))
    return jnp.concatenate(blocks, axis=1)


def _sigmoid(x):
    return 1.0 / (1.0 + jnp.exp(-x))


def _rw_prep_kernel(x_ref, xp_ref, xn_ref, wz_ref, mup_ref, mun_ref, w0_ref, w2h_ref, w2l_ref, a0_ref, a2h_ref,
                    a2l_ref, g2h_ref, g2l_ref, kk_w_ref, ka_w_ref, rk_w_ref, seg_ref,
                    r_ref, v_ref, kk_ref, lw_ref, kd_ref, be_ref, gate_ref, bonus_ref):
    i = pl.program_id(1)
    n = pl.num_programs(1)
    ts = x_ref.shape[1]
    halo = RW_HALO
    xe = jnp.concatenate([xp_ref[0], x_ref[0], xn_ref[0]], axis=0)
    ze = jnp.dot(xe, wz_ref[...], preferred_element_type=F32)
    z = ze[halo:halo + ts]
    slab = 8
    t = lax.broadcasted_iota(jnp.int32, (slab, 1), 0)
    prev_row = jnp.where(i > 0, ze[halo - 1:halo], 0.0)
    next_row = jnp.where(i < n - 1, ze[halo + ts:halo + ts + 1], 0.0)
    prev = pltpu.roll(z, 1, axis=0)
    prev = jnp.concatenate([jnp.where(t == 0, prev_row, prev[:slab]), prev[slab:]], axis=0)
    nxt = pltpu.roll(z, ts - 1, axis=0)
    nxt = jnp.concatenate([nxt[:ts - slab], jnp.where(t == slab - 1, next_row, nxt[ts - slab:])], axis=0)
    z = z + mup_ref[...] * (prev - z) + mun_ref[...] * (nxt - z)

    w = RW_WIDTH
    r = z[:, 0:w]
    k = z[:, w:2 * w]
    v = z[:, 2 * w:3 * w]
    lw = z[:, 3 * w:3 * w + 2 * DECAY_RANK]
    la = z[:, 3 * w + 2 * DECAY_RANK:3 * w + 2 * DECAY_RANK + 2 * ICLR_RANK]
    lg = z[:, 3 * w + 2 * DECAY_RANK + 2 * ICLR_RANK:]

    x = w0_ref[...] + _mm_hilo(jnp.tanh(lw), w2h_ref[...], w2l_ref[...])
    log_decay = -math.exp(-0.5) / (1.0 + jnp.exp(-x))
    iclr = _sigmoid(a0_ref[...] + _mm_hilo(la, a2h_ref[...], a2l_ref[...]))
    gate = _mm_hilo(_sigmoid(lg), g2h_ref[...], g2l_ref[...])

    seg = seg_ref[...]
    kk = k * kk_w_ref[...]
    kk = kk * lax.rsqrt(jnp.maximum(_seg_sum(kk * kk, seg), 1e-24))
    bonus = _seg_sum(r * k * rk_w_ref[...], seg) * v

    r_ref[0] = r
    v_ref[0] = v
    kk_ref[0] = kk
    lw_ref[0] = log_decay
    gate_ref[0] = gate
    bonus_ref[0] = bonus
    ka = ka_w_ref[...]
    for d in range(2):
        a_d = iclr[:, d * w:(d + 1) * w]
        kd_ref[0, :, d * w:(d + 1) * w] = k * (1.0 + (a_d - 1.0) * ka)
        be_ref[0, :, d * w:(d + 1) * w] = kk * a_d


def _block_diag2(m):
    z = jnp.zeros_like(m[0])
    return jnp.concatenate([jnp.concatenate([m[0], z], axis=1), jnp.concatenate([z, m[1]], axis=1)], axis=0)


def _rw_prep(xb, w_zr, mu_prev, mu_next, w0, w2, a0, a2, g2, k_k, k_a, r_k):
    b, s, dm = xb.shape
    c = w_zr.shape[1]
    ts = RW_PREP_TILE
    w = RW_WIDTH
    seg = _head_segment_matrix()
    halo = RW_HALO
    nhalo = s // halo

    def vec(width):
        return pl.BlockSpec((1, width), lambda bi, i: (0, 0))

    def mat(rows, cols):
        return pl.BlockSpec((rows, cols), lambda bi, i: (0, 0))

    def out(width):
        return pl.BlockSpec((1, ts, width), lambda bi, i: (bi, i, 0))

    shapes = [jax.ShapeDtypeStruct((b, s, width), F32) for width in (w, w, w, 2 * w, 2 * w, 2 * w, w, w)]
    return pl.pallas_call(
        _rw_prep_kernel,
        grid=(b, s // ts),
        in_specs=[pl.BlockSpec((1, ts, dm), lambda bi, i: (bi, i, 0)),
                  pl.BlockSpec((1, halo, dm), lambda bi, i: (bi, jnp.maximum(i * (ts // halo) - 1, 0), 0)),
                  pl.BlockSpec((1, halo, dm), lambda bi, i: (bi, jnp.minimum((i + 1) * (ts // halo), nhalo - 1), 0)),
                  mat(dm, c), vec(c), vec(c), vec(2 * w), mat(2 * DECAY_RANK, 2 * w), mat(2 * DECAY_RANK, 2 * w),
                  vec(2 * w), mat(2 * ICLR_RANK, 2 * w), mat(2 * ICLR_RANK, 2 * w),
                  mat(GATE_RANK, w), mat(GATE_RANK, w), vec(w), vec(w), vec(w), mat(SEG_LANES, SEG_LANES)],
        out_specs=[out(w), out(w), out(w), out(2 * w), out(2 * w), out(2 * w), out(w), out(w)],
        out_shape=shapes,
        compiler_params=_cparams("parallel", "arbitrary"),
        name="rw_prep",
    )(xb, xb, xb, w_zr, mu_prev.reshape(1, -1), mu_next.reshape(1, -1), w0.reshape(1, -1),
      *_split_bf16(_block_diag2(w2), 2),
      a0.reshape(1, -1), *_split_bf16(_block_diag2(a2), 2), *_split_bf16(g2, 2),
      k_k.reshape(1, -1), k_a.reshape(1, -1), r_k.reshape(1, -1), seg)


def _unit_triangular_inverses(mats):
    n = mats[0].shape[0]
    ti = lax.broadcasted_iota(jnp.int32, (n, n), 0)
    si = lax.broadcasted_iota(jnp.int32, (n, n), 1)

    def same_block(m):
        return (ti // m) == (si // m)

    mm = functools.partial(_mm, dims=NN, passes=INV_PASSES)
    eye = (ti == si).astype(F32)
    base = same_block(RW_INV_BASE)
    diag = [jnp.where(base, a, 0.0) for a in mats]
    power = [mm(a, a) for a in diag]
    inv = [mm(eye + a, eye + p) for a, p in zip(diag, power)]
    m = 4
    while m < RW_INV_BASE:
        power = [mm(p, p) for p in power]
        inv = [mm(t, eye + p) for t, p in zip(inv, power)]
        m *= 2
    m = RW_INV_BASE
    while m < n:
        ring = jnp.logical_and(same_block(2 * m), jnp.logical_not(same_block(m)))
        tmp = [mm(jnp.where(ring, a, 0.0), t) for a, t in zip(mats, inv)]
        inv = [t + mm(t, x) for t, x in zip(inv, tmp)]
        m *= 2
    return inv


def _rw_scan_kernel(rf_ref, vf_ref, kkf_ref, lwf_ref, kdf_ref, bef_ref,
                    rb_ref, vb_ref, kkb_ref, lwb_ref, kdb_ref, beb_ref, yf_ref, yb_ref, state_ref):
    @pl.when(pl.program_id(1) == 0)
    def _():
        state_ref[...] = jnp.zeros_like(state_ref)

    n = RW_CHUNK
    hd = RW_HEAD_DIM
    ti = lax.broadcasted_iota(jnp.int32, (2 * n, 2 * n), 0)
    si = lax.broadcasted_iota(jnp.int32, (2 * n, 2 * n), 1)
    zeros = jnp.zeros((n, hd), F32)

    chains = []
    for d, (r_ref, v_ref, kk_ref, lw_ref, kd_ref, be_ref) in enumerate((
            (rf_ref, vf_ref, kkf_ref, lwf_ref, kdf_ref, bef_ref),
            (rb_ref, vb_ref, kkb_ref, lwb_ref, kdb_ref, beb_ref))):
        lag = (ti % n - si % n) * (1 - 2 * d)
        keep = lag >= jnp.where(ti < n, 1, 0)
        tri = (lag[:n, :n] >= 0).astype(BF16)
        for bb in range(RW_BATCH_BLOCK):
            lw = lw_ref[bb]
            cum = sum(jnp.dot(tri, part, preferred_element_type=F32) for part in _split_bf16(lw, 3))
            e_neg = jnp.exp(-cum)
            ar_all = jnp.concatenate([-kk_ref[bb] * jnp.exp(cum - lw), r_ref[bb] * jnp.exp(cum)], axis=0)
            bk_all = jnp.concatenate([be_ref[bb] * e_neg, kd_ref[bb] * e_neg], axis=0)
            v_all = v_ref[bb]
            e_tot = jnp.exp(jnp.sum(lw, axis=0, keepdims=True))
            for h in range(RW_HEADS):
                cols = slice(h * hd, (h + 1) * hd)
                chains.append(dict(bb=bb, d=d, h=h, cols=cols, keep=keep, ar=ar_all[:, cols], bk=bk_all[:, cols],
                                   v=v_all[:, cols], decay=e_tot[:, cols], st=state_ref[bb, d, h]))

    for ch in chains:
        ch["m"] = jnp.where(ch["keep"], _mm(ch["ar"], ch["bk"], NT, MIX_PASSES), 0.0)
    for ch in chains:
        ch["xy"] = (_mm(ch["ar"], ch["st"], NT, MIX_PASSES)
                    + _mm(ch["m"], jnp.concatenate([zeros, ch["v"]], axis=0), NN, MIX_PASSES))
    invs = _unit_triangular_inverses([ch["m"][:n, :n] for ch in chains])
    for ch, inv in zip(chains, invs):
        ch["u"] = _mm(inv, ch["xy"][:n], NN, INV_PASSES)
    for ch in chains:
        y = ch["xy"][n:] + _mm(ch["m"][n:], jnp.concatenate([ch["u"], zeros], axis=0), NN, MIX_PASSES)
        (yf_ref, yb_ref)[ch["d"]][ch["bb"], :, ch["cols"]] = y
    for ch in chains:
        upd = _mm(jnp.concatenate([ch["u"], ch["v"]], axis=0), ch["bk"], TN, MIX_PASSES)
        state_ref[ch["bb"], ch["d"], ch["h"]] = (ch["st"] + upd) * ch["decay"]


def _rw_scan(r, v, kk, lw, kd, be):
    b, s, w = r.shape
    n = RW_CHUNK
    nc = s // n
    nb = RW_BATCH_BLOCK

    def spec(d, col):
        if d == 0:
            return pl.BlockSpec((nb, n, w), lambda bi, c: (bi, c, col))
        return pl.BlockSpec((nb, n, w), lambda bi, c: (bi, nc - 1 - c, col))

    in_specs = []
    for d in range(2):
        in_specs += [spec(d, 0), spec(d, 0), spec(d, 0), spec(d, d), spec(d, d), spec(d, d)]
    return pl.pallas_call(
        _rw_scan_kernel,
        grid=(b // nb, nc),
        in_specs=in_specs,
        out_specs=[spec(0, 0), spec(1, 0)],
        out_shape=[jax.ShapeDtypeStruct((b, s, w), F32)] * 2,
        scratch_shapes=[pltpu.VMEM((nb, 2, RW_HEADS, RW_HEAD_DIM, RW_HEAD_DIM), F32)],
        compiler_params=_cparams("parallel", "arbitrary"),
        name="rw_scan",
    )(r, v, kk, lw, kd, be, r, v, kk, lw, kd, be)


def _rw_out_kernel(yf_ref, yb_ref, gate_ref, bonus_ref, g_ref, b_ref, seg_ref, o_ref):
    y = yf_ref[0] + yb_ref[0]
    seg = seg_ref[...]
    mu = _seg_sum(y, seg) * (1.0 / RW_HEAD_DIM)
    yc = y - mu
    var = _seg_sum(yc * yc, seg) * (1.0 / RW_HEAD_DIM)
    yn = yc * lax.rsqrt(var + GN_EPS) * g_ref[...] + b_ref[...]
    o_ref[0] = ((yn + bonus_ref[0]) * gate_ref[0]).astype(BF16)


def _rw_out(yf, yb, gate, bonus, gn_g, gn_b):
    b, s, w = yf.shape
    ts = ROW_TILE
    seg = _head_segment_matrix()
    tile = pl.BlockSpec((1, ts, w), lambda bi, i: (bi, i, 0))
    vec = pl.BlockSpec((1, w), lambda bi, i: (0, 0))
    return pl.pallas_call(
        _rw_out_kernel,
        grid=(b, s // ts),
        in_specs=[tile, tile, tile, tile, vec, vec,
                  pl.BlockSpec((SEG_LANES, SEG_LANES), lambda bi, i: (0, 0))],
        out_specs=tile,
        out_shape=jax.ShapeDtypeStruct((b, s, w), BF16),
        compiler_params=_cparams("parallel", "parallel"),
        name="rw_out",
    )(yf, yb, gate, bonus, gn_g.reshape(1, -1), gn_b.reshape(1, -1), seg)


def _merge_kernel(ya_ref, yb_ref, yc_ref, wg_ref, wb_ref, bg_ref, wo_ref, x_ref, g_ref, b_ref, o_ref):
    x = x_ref[...]
    xb = x.astype(BF16)
    merged = None
    for n, y_ref in enumerate((ya_ref, yb_ref, yc_ref)):
        cols = slice(n * D_MODEL, (n + 1) * D_MODEL)
        gate_logits = jnp.dot(xb, wg_ref[:, cols], preferred_element_type=F32) + bg_ref[:, cols]
        branch = jnp.dot(y_ref[...], wb_ref[n], preferred_element_type=F32)
        term = _sigmoid(gate_logits) * branch
        merged = term if merged is None else merged + term
    mix = jnp.dot(merged.astype(BF16), wo_ref[...], preferred_element_type=F32)
    o_ref[...] = _layer_norm(ALPHA * x + mix, g_ref[...], b_ref[...])


def _merge(ya, yb, yc, w_gate, w_branch, b_gate, w_out, x, g, b):
    n = x.shape[0]
    tm = ROW_TILE
    ytile = pl.BlockSpec((tm, BRANCH_WIDTH), lambda i: (i, 0))
    row = pl.BlockSpec((tm, D_MODEL), lambda i: (i, 0))
    vec = pl.BlockSpec((1, D_MODEL), lambda i: (0, 0))
    return pl.pallas_call(
        _merge_kernel,
        grid=(n // tm,),
        in_specs=[ytile, ytile, ytile,
                  pl.BlockSpec((D_MODEL, 3 * D_MODEL), lambda i: (0, 0)),
                  pl.BlockSpec((3, BRANCH_WIDTH, D_MODEL), lambda i: (0, 0, 0)),
                  pl.BlockSpec((1, 3 * D_MODEL), lambda i: (0, 0)),
                  pl.BlockSpec((D_MODEL, D_MODEL), lambda i: (0, 0)),
                  row, vec, vec],
        out_specs=row,
        out_shape=jax.ShapeDtypeStruct((n, D_MODEL), F32),
        compiler_params=_cparams("parallel"),
        name="merge",
    )(ya, yb, yc, w_gate, w_branch.astype(BF16), b_gate.reshape(1, -1), w_out.astype(BF16), x,
      g.reshape(1, -1), b.reshape(1, -1))


def _mlp_kernel(x_ref, wu_ref, wd_ref, g_ref, b_ref, o_ref, ob_ref):
    x = x_ref[...]
    xb = x.astype(BF16)
    acc = None
    for j in range(D_FF // MLP_FF_SLICE):
        ff = slice(j * MLP_FF_SLICE, (j + 1) * MLP_FF_SLICE)
        hid = jnp.maximum(jnp.dot(xb, wu_ref[:, ff], preferred_element_type=F32), 0.0)
        part = jnp.dot((hid * hid).astype(BF16), wd_ref[ff, :], preferred_element_type=F32)
        acc = part if acc is None else acc + part
    y = _layer_norm(ALPHA * x + acc, g_ref[...], b_ref[...])
    o_ref[...] = y
    ob_ref[...] = y.astype(BF16)


def _mlp(x, w_up, w_down, g, b):
    n = x.shape[0]
    tm = MLP_ROW_TILE
    row = pl.BlockSpec((tm, D_MODEL), lambda i: (i, 0))
    vec = pl.BlockSpec((1, D_MODEL), lambda i: (0, 0))
    resident = dict(pipeline_mode=pl.Buffered(1))
    return pl.pallas_call(
        _mlp_kernel,
        grid=(n // tm,),
        in_specs=[row,
                  pl.BlockSpec((D_MODEL, D_FF), lambda i: (0, 0), **resident),
                  pl.BlockSpec((D_FF, D_MODEL), lambda i: (0, 0), **resident),
                  vec, vec],
        out_specs=[row, row],
        out_shape=[jax.ShapeDtypeStruct((n, D_MODEL), F32), jax.ShapeDtypeStruct((n, D_MODEL), BF16)],
        compiler_params=_cparams("parallel"),
        name="mlp",
    )(x, w_up.astype(BF16), w_down.astype(BF16), g.reshape(1, -1), b.reshape(1, -1))


def kernel(x, ln0_g, ln0_b, w_in, pool_w, pool_scale, da_lambda, da_subln_g, rel_bias, rw_mu_prev, rw_mu_next, rw_w0, rw_w2, rw_a0, rw_a2, rw_g2, rw_k_k, rw_k_a, rw_r_k, rw_gn_g, rw_gn_b, w_branch, b_gate, w_out, ln1_g, ln1_b, w_up, w_down, ln2_g, ln2_b):
    b, s, dm = x.shape
    n = b * s
    assert dm == D_MODEL and n % MLP_ROW_TILE == 0 and n % ROW_TILE == 0
    assert s % ATTN_Q_TILE == 0 and s % ROW_TILE == 0 and s % RW_PREP_TILE == 0
    assert b % RW_BATCH_BLOCK == 0 and s % RW_CHUNK == 0
    c_uq = 2 * BRANCH_WIDTH
    c_uqkv = 4 * BRANCH_WIDTH
    c_zr = c_uqkv + RW_SHIFT_WIDTH

    strip = _bias_strip(rel_bias, s)
    xf, xb = _ln0(x.reshape(n, dm), ln0_g, ln0_b)
    for l in range(DEPTH):
        wl = w_in[l].astype(BF16)
        uq = _proj(xb, wl[:, :c_uq], c_uq, "proj_uq").reshape(b, s, c_uq)
        kv = _proj(xb, wl[:, c_uq:c_uqkv], c_uq, "proj_kv", BF16).reshape(b, s, c_uq)

        y_a = _pool(uq, pool_w[l], pool_scale[l])
        lam_init = 0.8 - 0.6 * math.exp(-0.3 * l)
        y_b = _diff_attention(uq, kv, strip, da_lambda[l], da_subln_g[l], lam_init)
        r, v, kk, lw, kd, be, gate, bonus = _rw_prep(
            xb.reshape(b, s, dm), wl[:, c_uqkv:c_zr], rw_mu_prev[l], rw_mu_next[l], rw_w0[l], rw_w2[l], rw_a0[l], rw_a2[l], rw_g2[l],
            rw_k_k[l], rw_k_a[l], rw_r_k[l])
        yf, yb = _rw_scan(r, v, kk, lw, kd, be)
        y_c = _rw_out(yf, yb, gate, bonus, rw_gn_g[l], rw_gn_b[l])

        xf = _merge(y_a.reshape(n, -1), y_b.reshape(n, -1), y_c.reshape(n, -1), wl[:, c_zr:],
                    w_branch[l], b_gate[l], w_out[l], xf, ln1_g[l], ln1_b[l])
        xf, xb = _mlp(xf, w_up[l], w_down[l], ln2_g[l], ln2_b[l])
    return xf.reshape(b, s, dm)
```

```python
import functools
import math

import numpy as np
import jax
import jax.numpy as jnp
from jax import lax
from jax.experimental import pallas as pl
from jax.experimental.pallas import tpu as pltpu

D_MODEL = 1024
DEPTH = 4
BRANCH_WIDTH = 512
POOL_WINDOWS = (2, 4, 8, 16)
POOL_GROUP_DIM = 128
DA_HEAD_DIM = 64
DA_HEADS = 4
REL_BUCKETS = 32
REL_MAX_DIST = 128
RW_WIDTH = 512
RW_HEAD_DIM = 64
RW_HEADS = 8
DECAY_RANK = 64
ICLR_RANK = 64
GATE_RANK = 128
RW_SHIFT_WIDTH = 3 * RW_WIDTH + 2 * DECAY_RANK + 2 * ICLR_RANK + GATE_RANK
GN_EPS = 64e-5
D_FF = 4 * D_MODEL
LN_EPS = 1e-5
ALPHA = (2.0 * DEPTH) ** 0.25
LOG2E = math.log2(math.e)

VMEM_LIMIT_BYTES = 56 * 1024 * 1024

ROW_TILE = 512
MLP_ROW_TILE = 512
MLP_FF_SLICE = 1024
ATTN_Q_TILE = 256
ATTN_HEADS_PER_STEP = 4
RW_PREP_TILE = 512
RW_HALO = 16
RW_CHUNK = 64
RW_BATCH_BLOCK = 2
RW_INV_BASE = 8
INV_PASSES = 1
MIX_PASSES = 1

F32 = jnp.float32
BF16 = jnp.bfloat16
NN = (((1,), (0,)), ((), ()))
NT = (((1,), (1,)), ((), ()))
TN = (((0,), (0,)), ((), ()))


def _cparams(*sem):
    return pltpu.CompilerParams(dimension_semantics=sem, vmem_limit_bytes=VMEM_LIMIT_BYTES)


def _layer_norm(x, g, b):
    mu = jnp.mean(x, axis=-1, keepdims=True)
    xc = x - mu
    var = jnp.mean(xc * xc, axis=-1, keepdims=True)
    return xc * lax.rsqrt(var + LN_EPS) * g + b


def _split_bf16(x, terms):
    parts = []
    for _ in range(terms):
        p = x.astype(BF16)
        parts.append(p)
        x = x - p.astype(F32)
    return parts


def _mm_hilo(a, b_hi, b_lo):
    ah, al = _split_bf16(a, 2)
    return (jnp.dot(ah, b_hi, preferred_element_type=F32) + jnp.dot(ah, b_lo, preferred_element_type=F32)
            + jnp.dot(al, b_hi, preferred_element_type=F32))


def _mm(a, b, dims, passes):
    if passes == 1:
        return lax.dot_general(a.astype(BF16), b.astype(BF16), dims, preferred_element_type=F32)
    assert passes == 3 and dims == NN
    return _mm_hilo(a, *_split_bf16(b, 2))


def _ln0_kernel(x_ref, g_ref, b_ref, o_ref, ob_ref):
    y = _layer_norm(x_ref[...], g_ref[...], b_ref[...])
    o_ref[...] = y
    ob_ref[...] = y.astype(BF16)


def _ln0(x2, g, b):
    n = x2.shape[0]
    row = pl.BlockSpec((ROW_TILE, D_MODEL), lambda i: (i, 0))
    vec = pl.BlockSpec((1, D_MODEL), lambda i: (0, 0))
    return pl.pallas_call(
        _ln0_kernel,
        grid=(n // ROW_TILE,),
        in_specs=[row, vec, vec],
        out_specs=[row, row],
        out_shape=[jax.ShapeDtypeStruct((n, D_MODEL), F32), jax.ShapeDtypeStruct((n, D_MODEL), BF16)],
        compiler_params=_cparams("parallel"),
        name="ln0",
    )(x2, g.reshape(1, -1), b.reshape(1, -1))


def _proj_kernel(x_ref, w_ref, o_ref):
    o_ref[...] = jnp.dot(x_ref[...], w_ref[...], preferred_element_type=F32).astype(o_ref.dtype)


def _proj(xb, w, tn, name, out_dtype=F32):
    n, k = xb.shape
    m = w.shape[1]
    return pl.pallas_call(
        _proj_kernel,
        grid=(n // ROW_TILE, m // tn),
        in_specs=[pl.BlockSpec((ROW_TILE, k), lambda i, j: (i, 0)),
                  pl.BlockSpec((k, tn), lambda i, j: (0, j))],
        out_specs=pl.BlockSpec((ROW_TILE, tn), lambda i, j: (i, j)),
        out_shape=jax.ShapeDtypeStruct((n, m), out_dtype),
        compiler_params=_cparams("parallel", "arbitrary"),
        name=name,
    )(xb, w)


def _pool_kernel(u_ref, w_ref, s_ref, o_ref):
    s = u_ref.shape[1]
    t = lax.broadcasted_iota(jnp.int32, (s, POOL_GROUP_DIM), 0)
    for gi, win in enumerate(POOL_WINDOWS):
        left = win // 2
        right = win - 1 - left
        cols = slice(gi * POOL_GROUP_DIM, (gi + 1) * POOL_GROUP_DIM)
        x = u_ref[0, :, cols]
        acc = x
        for d in range(-left, right + 1):
            if d == 0:
                continue
            shifted = pltpu.roll(x, (-d) % s, axis=0)
            valid = jnp.logical_and(t + d >= 0, t + d < s)
            acc = acc + jnp.where(valid, shifted, 0.0)
        cnt = (jnp.minimum(t + right + 1, s) - jnp.maximum(t - left, 0)).astype(F32)
        mixed = acc / cnt - x
        y = jnp.dot(mixed.astype(BF16), w_ref[gi], preferred_element_type=F32)
        o_ref[0, :, cols] = (y * s_ref[:, cols]).astype(BF16)


def _pool(uq, pool_w, pool_scale):
    b, s, _ = uq.shape
    return pl.pallas_call(
        _pool_kernel,
        grid=(b,),
        in_specs=[pl.BlockSpec((1, s, BRANCH_WIDTH), lambda i: (i, 0, 0)),
                  pl.BlockSpec((len(POOL_WINDOWS), POOL_GROUP_DIM, POOL_GROUP_DIM), lambda i: (0, 0, 0)),
                  pl.BlockSpec((1, BRANCH_WIDTH), lambda i: (0, 0))],
        out_specs=pl.BlockSpec((1, s, BRANCH_WIDTH), lambda i: (i, 0, 0)),
        out_shape=jax.ShapeDtypeStruct((b, s, BRANCH_WIDTH), BF16),
        compiler_params=_cparams("parallel"),
        name="pool",
    )(uq, pool_w.astype(BF16), pool_scale.reshape(1, -1))


def _rel_bucket_np(rel):
    half = REL_BUCKETS // 2
    max_exact = half // 2
    n = np.abs(rel)
    nf = np.maximum(n, 1).astype(np.float64)
    large = max_exact + (np.log(nf / max_exact) / math.log(REL_MAX_DIST / max_exact)
                         * (half - max_exact)).astype(np.int32)
    large = np.minimum(large, half - 1)
    return (np.where(rel > 0, half, 0) + np.where(n < max_exact, n, large)).astype(np.int32)


def _bias_strip_kernel(tab_ref, bk_ref, o_ref):
    h = pl.program_id(0)
    bk = bk_ref[...]
    acc = jnp.zeros(bk.shape, F32)
    for j in range(REL_BUCKETS):
        acc = jnp.where(bk == j, tab_ref[j * DA_HEADS + h] * LOG2E, acc)
    o_ref[0] = acc


def _bias_strip(rel_bias, s):
    tq = ATTN_Q_TILE
    w = 2 * s - tq
    rel = np.arange(w)[None, :] - np.arange(tq)[:, None] - (s - tq)
    bucket = jnp.asarray(_rel_bucket_np(rel))
    return pl.pallas_call(
        _bias_strip_kernel,
        grid=(DA_HEADS,),
        in_specs=[pl.BlockSpec(memory_space=pltpu.SMEM),
                  pl.BlockSpec((tq, w), lambda h: (0, 0))],
        out_specs=pl.BlockSpec((1, tq, w), lambda h: (h, 0, 0)),
        out_shape=jax.ShapeDtypeStruct((DA_HEADS, tq, w), F32),
        compiler_params=_cparams("arbitrary"),
        name="bias_strip",
    )(rel_bias.reshape(-1), bucket)


def _attn_kernel(lam0_ref, lamv_ref, q_ref, k_ref, v_ref, strip_ref, g_ref, o_ref):
    tq = q_ref.shape[1]
    s = k_ref.shape[1]
    hw = 2 * DA_HEAD_DIM
    qi = pl.program_id(2)
    nq = pl.num_programs(2)
    lam_init = lam0_ref[0]
    lv = lamv_ref[...]
    lam = (jnp.exp(jnp.sum(lv[0:1] * lv[1:2], axis=-1, keepdims=True))
           - jnp.exp(jnp.sum(lv[2:3] * lv[3:4], axis=-1, keepdims=True)) + lam_init)
    off = pl.multiple_of((nq - 1 - qi) * tq, tq)
    lane = lax.broadcasted_iota(jnp.int32, (tq, hw), 1)

    def logits(h, first_half):
        cols = slice(h * hw, (h + 1) * hw)
        q = q_ref[0, :, cols] * (DA_HEAD_DIM ** -0.5 * LOG2E)
        qm = jnp.where((lane < DA_HEAD_DIM) == first_half, q, 0.0).astype(BF16)
        return (lax.dot_general(qm, k_ref[0, :, cols], NT, preferred_element_type=F32)
                + strip_ref[h, :, pl.ds(off, s)])

    def softmax_times_v(h, lg):
        m = jnp.max(lg, axis=-1, keepdims=True)
        e = jnp.exp2(lg - m)
        l = jnp.sum(e, axis=-1, keepdims=True)
        return jnp.dot(e.astype(BF16), v_ref[0, :, h * hw:(h + 1) * hw], preferred_element_type=F32) / l

    streams = [(h, first) for h in range(ATTN_HEADS_PER_STEP) for first in (True, False)]
    pending = logits(*streams[0])
    outs = []
    for idx, (h, _) in enumerate(streams):
        lg = pending
        if idx + 1 < len(streams):
            pending = logits(*streams[idx + 1])
        outs.append(softmax_times_v(h, lg))
    for h in range(ATTN_HEADS_PER_STEP):
        o = outs[2 * h] - lam * outs[2 * h + 1]
        o = o * lax.rsqrt(jnp.mean(o * o, axis=-1, keepdims=True) + 1e-5) * g_ref[...]
        o_ref[0, :, h * hw:(h + 1) * hw] = (o * (1.0 - lam_init)).astype(BF16)


def _diff_attention(uq, kv, strip, da_lambda, subln_g, lam_init):
    b, s, _ = uq.shape
    tq = ATTN_Q_TILE
    nh = ATTN_HEADS_PER_STEP
    hw = 2 * DA_HEAD_DIM
    bw = nh * hw
    q_blk0 = BRANCH_WIDTH // bw
    v_blk0 = BRANCH_WIDTH // bw
    return pl.pallas_call(
        _attn_kernel,
        grid=(b, DA_HEADS // nh, s // tq),
        in_specs=[pl.BlockSpec(memory_space=pltpu.SMEM),
                  pl.BlockSpec((4, DA_HEAD_DIM), lambda bi, hp, qi: (0, 0)),
                  pl.BlockSpec((1, tq, bw), lambda bi, hp, qi: (bi, qi, q_blk0 + hp)),
                  pl.BlockSpec((1, s, bw), lambda bi, hp, qi: (bi, 0, hp)),
                  pl.BlockSpec((1, s, bw), lambda bi, hp, qi: (bi, 0, v_blk0 + hp)),
                  pl.BlockSpec((nh, tq, strip.shape[2]), lambda bi, hp, qi: (hp, 0, 0),
                               pipeline_mode=pl.Buffered(1 if nh == DA_HEADS else 2)),
                  pl.BlockSpec((1, hw), lambda bi, hp, qi: (0, 0))],
        out_specs=pl.BlockSpec((1, tq, bw), lambda bi, hp, qi: (bi, qi, hp)),
        out_shape=jax.ShapeDtypeStruct((b, s, BRANCH_WIDTH), BF16),
        compiler_params=_cparams("parallel", "parallel", "arbitrary"),
        name="diff_attn",
    )(jnp.full((1,), lam_init, F32), da_lambda, uq, kv, kv, strip, subln_g.reshape(1, -1))


SEG_LANES = 256


def _head_segment_matrix():
    heads = SEG_LANES // RW_HEAD_DIM
    return jnp.asarray(np.kron(np.eye(heads), np.ones((RW_HEAD_DIM, RW_HEAD_DIM))), dtype=BF16)


def _seg_sum(x, seg):
    parts = _split_bf16(x, 2)
    blocks = []
    for c in range(x.shape[1] // SEG_LANES):
        cols = slice(c * SEG_LANES, (c + 1) * SEG_LANES)
        blocks.append(sum(jnp.dot(p[:, cols], seg, preferred_element_type=F32) for p in parts))
    return jnp.concatenate(blocks, axis=1)


def _sigmoid(x):
    return 1.0 / (1.0 + jnp.exp(-x))


def _rw_prep_kernel(x_ref, xp_ref, xn_ref, wz_ref, mup_ref, mun_ref, w0_ref, w2h_ref, w2l_ref, a0_ref, a2h_ref,
                    a2l_ref, g2h_ref, g2l_ref, kk_w_ref, ka_w_ref, rk_w_ref, seg_ref,
                    r_ref, v_ref, kk_ref, lw_ref, kd_ref, be_ref, gate_ref, bonus_ref):
    i = pl.program_id(1)
    n = pl.num_programs(1)
    ts = x_ref.shape[1]
    halo = RW_HALO
    xe = jnp.concatenate([xp_ref[0], x_ref[0], xn_ref[0]], axis=0)
    ze = jnp.dot(xe, wz_ref[...], preferred_element_type=F32)
    z = ze[halo:halo + ts]
    slab = 8
    t = lax.broadcasted_iota(jnp.int32, (slab, 1), 0)
    prev_row = jnp.where(i > 0, ze[halo - 1:halo], 0.0)
    next_row = jnp.where(i < n - 1, ze[halo + ts:halo + ts + 1], 0.0)
    prev = pltpu.roll(z, 1, axis=0)
    prev = jnp.concatenate([jnp.where(t == 0, prev_row, prev[:slab]), prev[slab:]], axis=0)
    nxt = pltpu.roll(z, ts - 1, axis=0)
    nxt = jnp.concatenate([nxt[:ts - slab], jnp.where(t == slab - 1, next_row, nxt[ts - slab:])], axis=0)
    z = z + mup_ref[...] * (prev - z) + mun_ref[...] * (nxt - z)

    w = RW_WIDTH
    r = z[:, 0:w]
    k = z[:, w:2 * w]
    v = z[:, 2 * w:3 * w]
    lw = z[:, 3 * w:3 * w + 2 * DECAY_RANK]
    la = z[:, 3 * w + 2 * DECAY_RANK:3 * w + 2 * DECAY_RANK + 2 * ICLR_RANK]
    lg = z[:, 3 * w + 2 * DECAY_RANK + 2 * ICLR_RANK:]

    x = w0_ref[...] + _mm_hilo(jnp.tanh(lw), w2h_ref[...], w2l_ref[...])
    log_decay = -math.exp(-0.5) / (1.0 + jnp.exp(-x))
    iclr = _sigmoid(a0_ref[...] + _mm_hilo(la, a2h_ref[...], a2l_ref[...]))
    gate = _mm_hilo(_sigmoid(lg), g2h_ref[...], g2l_ref[...])

    seg = seg_ref[...]
    kk = k * kk_w_ref[...]
    kk = kk * lax.rsqrt(jnp.maximum(_seg_sum(kk * kk, seg), 1e-24))
    bonus = _seg_sum(r * k * rk_w_ref[...], seg) * v

    r_ref[0] = r
    v_ref[0] = v
    kk_ref[0] = kk
    lw_ref[0] = log_decay
    gate_ref[0] = gate
    bonus_ref[0] = bonus
    ka = ka_w_ref[...]
    for d in range(2):
        a_d = iclr[:, d * w:(d + 1) * w]
        kd_ref[0, :, d * w:(d + 1) * w] = k * (1.0 + (a_d - 1.0) * ka)
        be_ref[0, :, d * w:(d + 1) * w] = kk * a_d


def _block_diag2(m):
    z = jnp.zeros_like(m[0])
    return jnp.concatenate([jnp.concatenate([m[0], z], axis=1), jnp.concatenate([z, m[1]], axis=1)], axis=0)


def _rw_prep(xb, w_zr, mu_prev, mu_next, w0, w2, a0, a2, g2, k_k, k_a, r_k):
    b, s, dm = xb.shape
    c = w_zr.shape[1]
    ts = RW_PREP_TILE
    w = RW_WIDTH
    seg = _head_segment_matrix()
    halo = RW_HALO
    nhalo = s // halo

    def vec(width):
        return pl.BlockSpec((1, width), lambda bi, i: (0, 0))

    def mat(rows, cols):
        return pl.BlockSpec((rows, cols), lambda bi, i: (0, 0))

    def out(width):
        return pl.BlockSpec((1, ts, width), lambda bi, i: (bi, i, 0))

    shapes = [jax.ShapeDtypeStruct((b, s, width), F32) for width in (w, w, w, 2 * w, 2 * w, 2 * w, w, w)]
    return pl.pallas_call(
        _rw_prep_kernel,
        grid=(b, s // ts),
        in_specs=[pl.BlockSpec((1, ts, dm), lambda bi, i: (bi, i, 0)),
                  pl.BlockSpec((1, halo, dm), lambda bi, i: (bi, jnp.maximum(i * (ts // halo) - 1, 0), 0)),
                  pl.BlockSpec((1, halo, dm), lambda bi, i: (bi, jnp.minimum((i + 1) * (ts // halo), nhalo - 1), 0)),
                  mat(dm, c), vec(c), vec(c), vec(2 * w), mat(2 * DECAY_RANK, 2 * w), mat(2 * DECAY_RANK, 2 * w),
                  vec(2 * w), mat(2 * ICLR_RANK, 2 * w), mat(2 * ICLR_RANK, 2 * w),
                  mat(GATE_RANK, w), mat(GATE_RANK, w), vec(w), vec(w), vec(w), mat(SEG_LANES, SEG_LANES)],
        out_specs=[out(w), out(w), out(w), out(2 * w), out(2 * w), out(2 * w), out(w), out(w)],
        out_shape=shapes,
        compiler_params=_cparams("parallel", "arbitrary"),
        name="rw_prep",
    )(xb, xb, xb, w_zr, mu_prev.reshape(1, -1), mu_next.reshape(1, -1), w0.reshape(1, -1),
      *_split_bf16(_block_diag2(w2), 2),
      a0.reshape(1, -1), *_split_bf16(_block_diag2(a2), 2), *_split_bf16(g2, 2),
      k_k.reshape(1, -1), k_a.reshape(1, -1), r_k.reshape(1, -1), seg)


def _unit_triangular_inverses(mats):
    n = mats[0].shape[0]
    ti = lax.broadcasted_iota(jnp.int32, (n, n), 0)
    si = lax.broadcasted_iota(jnp.int32, (n, n), 1)

    def same_block(m):
        return (ti // m) == (si // m)

    mm = functools.partial(_mm, dims=NN, passes=INV_PASSES)
    eye = (ti == si).astype(F32)
    base = same_block(RW_INV_BASE)
    diag = [jnp.where(base, a, 0.0) for a in mats]
    power = [mm(a, a) for a in diag]
    inv = [mm(eye + a, eye + p) for a, p in zip(diag, power)]
    m = 4
    while m < RW_INV_BASE:
        power = [mm(p, p) for p in power]
        inv = [mm(t, eye + p) for t, p in zip(inv, power)]
        m *= 2
    m = RW_INV_BASE
    while m < n:
        ring = jnp.logical_and(same_block(2 * m), jnp.logical_not(same_block(m)))
        tmp = [mm(jnp.where(ring, a, 0.0), t) for a, t in zip(mats, inv)]
        inv = [t + mm(t, x) for t, x in zip(inv, tmp)]
        m *= 2
    return inv


def _rw_scan_kernel(rf_ref, vf_ref, kkf_ref, lwf_ref, kdf_ref, bef_ref,
                    rb_ref, vb_ref, kkb_ref, lwb_ref, kdb_ref, beb_ref, yf_ref, yb_ref, state_ref):
    @pl.when(pl.program_id(1) == 0)
    def _():
        state_ref[...] = jnp.zeros_like(state_ref)

    n = RW_CHUNK
    hd = RW_HEAD_DIM
    ti = lax.broadcasted_iota(jnp.int32, (2 * n, 2 * n), 0)
    si = lax.broadcasted_iota(jnp.int32, (2 * n, 2 * n), 1)

    chains = []
    for d, (r_ref, v_ref, kk_ref, lw_ref, kd_ref, be_ref) in enumerate((
            (rf_ref, vf_ref, kkf_ref, lwf_ref, kdf_ref, bef_ref),
            (rb_ref, vb_ref, kkb_ref, lwb_ref, kdb_ref, beb_ref))):
        lag = (ti % n - si % n) * (1 - 2 * d)
        keep = lag >= jnp.where(ti < n, 1, 0)
        tri = (lag[:n, :n] >= 0).astype(BF16)
        for bb in range(RW_BATCH_BLOCK):
            lw = lw_ref[bb]
            cum = sum(jnp.dot(tri, part, preferred_element_type=F32) for part in _split_bf16(lw, 3))
            e_neg = jnp.exp(-cum)
            ar_all = jnp.concatenate([-kk_ref[bb] * jnp.exp(cum - lw), r_ref[bb] * jnp.exp(cum)], axis=0)
            bk_all = jnp.concatenate([be_ref[bb] * e_neg, kd_ref[bb] * e_neg], axis=0)
            v_all = v_ref[bb]
            e_tot = jnp.exp(jnp.sum(lw, axis=0, keepdims=True))
            for h in range(RW_HEADS):
                cols = slice(h * hd, (h + 1) * hd)
                chains.append(dict(bb=bb, d=d, h=h, cols=cols, keep=keep, ar=ar_all[:, cols], bk=bk_all[:, cols],
                                   v=v_all[:, cols], decay=e_tot[:, cols], st=state_ref[bb, d, h]))

    for ch in chains:
        ch["m"] = jnp.where(ch["keep"], _mm(ch["ar"], ch["bk"], NT, MIX_PASSES), 0.0)
    for ch in chains:
        ch["xy"] = _mm(ch["ar"], ch["st"], NT, MIX_PASSES) + _mm(ch["m"][:, n:], ch["v"], NN, MIX_PASSES)
    invs = _unit_triangular_inverses([ch["m"][:n, :n] for ch in chains])
    for ch, inv in zip(chains, invs):
        ch["u"] = _mm(inv, ch["xy"][:n], NN, INV_PASSES)
    for ch in chains:
        y = ch["xy"][n:] + _mm(ch["m"][n:, :n], ch["u"], NN, MIX_PASSES)
        (yf_ref, yb_ref)[ch["d"]][ch["bb"], :, ch["cols"]] = y
    for ch in chains:
        upd = _mm(jnp.concatenate([ch["u"], ch["v"]], axis=0), ch["bk"], TN, MIX_PASSES)
        state_ref[ch["bb"], ch["d"], ch["h"]] = (ch["st"] + upd) * ch["decay"]


def _rw_scan(r, v, kk, lw, kd, be):
    b, s, w = r.shape
    n = RW_CHUNK
    nc = s // n
    nb = RW_BATCH_BLOCK

    def spec(d, col):
        if d == 0:
            return pl.BlockSpec((nb, n, w), lambda bi, c: (bi, c, col))
        return pl.BlockSpec((nb, n, w), lambda bi, c: (bi, nc - 1 - c, col))

    in_specs = []
    for d in range(2):
        in_specs += [spec(d, 0), spec(d, 0), spec(d, 0), spec(d, d), spec(d, d), spec(d, d)]
    return pl.pallas_call(
        _rw_scan_kernel,
        grid=(b // nb, nc),
        in_specs=in_specs,
        out_specs=[spec(0, 0), spec(1, 0)],
        out_shape=[jax.ShapeDtypeStruct((b, s, w), F32)] * 2,
        scratch_shapes=[pltpu.VMEM((nb, 2, RW_HEADS, RW_HEAD_DIM, RW_HEAD_DIM), F32)],
        compiler_params=_cparams("parallel", "arbitrary"),
        name="rw_scan",
    )(r, v, kk, lw, kd, be, r, v, kk, lw, kd, be)


def _rw_finish(yf, yb, gate, bonus, gn_g, gn_b, seg):
    y = yf + yb
    mu = _seg_sum(y, seg) * (1.0 / RW_HEAD_DIM)
    yc = y - mu
    var = _seg_sum(yc * yc, seg) * (1.0 / RW_HEAD_DIM)
    yn = yc * lax.rsqrt(var + GN_EPS) * gn_g + gn_b
    return ((yn + bonus) * gate).astype(BF16)


def _merge_kernel(ya_ref, yb_ref, yf_ref, yr_ref, rgate_ref, bonus_ref, gn_g_ref, gn_b_ref, seg_ref,
                  wg_ref, wb_ref, bg_ref, wo_ref, x_ref, g_ref, b_ref, o_ref):
    x = x_ref[...]
    xb = x.astype(BF16)
    yc = _rw_finish(yf_ref[...], yr_ref[...], rgate_ref[...], bonus_ref[...], gn_g_ref[...], gn_b_ref[...],
                    seg_ref[...])
    merged = None
    for n, y in enumerate((ya_ref[...], yb_ref[...], yc)):
        cols = slice(n * D_MODEL, (n + 1) * D_MODEL)
        gate_logits = jnp.dot(xb, wg_ref[:, cols], preferred_element_type=F32) + bg_ref[:, cols]
        branch = jnp.dot(y, wb_ref[n], preferred_element_type=F32)
        term = _sigmoid(gate_logits) * branch
        merged = term if merged is None else merged + term
    mix = jnp.dot(merged.astype(BF16), wo_ref[...], preferred_element_type=F32)
    o_ref[...] = _layer_norm(ALPHA * x + mix, g_ref[...], b_ref[...])


def _merge(ya, yb, yf, yr, rgate, bonus, gn_g, gn_b, w_gate, w_branch, b_gate, w_out, x, g, b):
    n = x.shape[0]
    tm = ROW_TILE
    ytile = pl.BlockSpec((tm, BRANCH_WIDTH), lambda i: (i, 0))
    row = pl.BlockSpec((tm, D_MODEL), lambda i: (i, 0))
    vec = pl.BlockSpec((1, D_MODEL), lambda i: (0, 0))
    hvec = pl.BlockSpec((1, RW_WIDTH), lambda i: (0, 0))
    return pl.pallas_call(
        _merge_kernel,
        grid=(n // tm,),
        in_specs=[ytile, ytile, ytile, ytile, ytile, ytile, hvec, hvec,
                  pl.BlockSpec((SEG_LANES, SEG_LANES), lambda i: (0, 0)),
                  pl.BlockSpec((D_MODEL, 3 * D_MODEL), lambda i: (0, 0)),
                  pl.BlockSpec((3, BRANCH_WIDTH, D_MODEL), lambda i: (0, 0, 0)),
                  pl.BlockSpec((1, 3 * D_MODEL), lambda i: (0, 0)),
                  pl.BlockSpec((D_MODEL, D_MODEL), lambda i: (0, 0)),
                  row, vec, vec],
        out_specs=row,
        out_shape=jax.ShapeDtypeStruct((n, D_MODEL), F32),
        compiler_params=_cparams("parallel"),
        name="merge",
    )(ya, yb, yf, yr, rgate, bonus, gn_g.reshape(1, -1), gn_b.reshape(1, -1), _head_segment_matrix(),
      w_gate, w_branch.astype(BF16), b_gate.reshape(1, -1), w_out.astype(BF16), x, g.reshape(1, -1), b.reshape(1, -1))


def _mlp_kernel(x_ref, wu_ref, wd_ref, g_ref, b_ref, o_ref, ob_ref):
    x = x_ref[...]
    xb = x.astype(BF16)
    acc = None
    for j in range(D_FF // MLP_FF_SLICE):
        ff = slice(j * MLP_FF_SLICE, (j + 1) * MLP_FF_SLICE)
        hid = jnp.maximum(jnp.dot(xb, wu_ref[:, ff], preferred_element_type=F32), 0.0)
        part = jnp.dot((hid * hid).astype(BF16), wd_ref[ff, :], preferred_element_type=F32)
        acc = part if acc is None else acc + part
    y = _layer_norm(ALPHA * x + acc, g_ref[...], b_ref[...])
    o_ref[...] = y
    ob_ref[...] = y.astype(BF16)


def _mlp(x, w_up, w_down, g, b):
    n = x.shape[0]
    tm = MLP_ROW_TILE
    row = pl.BlockSpec((tm, D_MODEL), lambda i: (i, 0))
    vec = pl.BlockSpec((1, D_MODEL), lambda i: (0, 0))
    resident = dict(pipeline_mode=pl.Buffered(1))
    return pl.pallas_call(
        _mlp_kernel,
        grid=(n // tm,),
        in_specs=[row,
                  pl.BlockSpec((D_MODEL, D_FF), lambda i: (0, 0), **resident),
                  pl.BlockSpec((D_FF, D_MODEL), lambda i: (0, 0), **resident),
                  vec, vec],
        out_specs=[row, row],
        out_shape=[jax.ShapeDtypeStruct((n, D_MODEL), F32), jax.ShapeDtypeStruct((n, D_MODEL), BF16)],
        compiler_params=_cparams("parallel"),
        name="mlp",
    )(x, w_up.astype(BF16), w_down.astype(BF16), g.reshape(1, -1), b.reshape(1, -1))


def kernel(x, ln0_g, ln0_b, w_in, pool_w, pool_scale, da_lambda, da_subln_g, rel_bias, rw_mu_prev, rw_mu_next, rw_w0, rw_w2, rw_a0, rw_a2, rw_g2, rw_k_k, rw_k_a, rw_r_k, rw_gn_g, rw_gn_b, w_branch, b_gate, w_out, ln1_g, ln1_b, w_up, w_down, ln2_g, ln2_b):
    b, s, dm = x.shape
    n = b * s
    assert dm == D_MODEL and n % MLP_ROW_TILE == 0 and n % ROW_TILE == 0
    assert s % ATTN_Q_TILE == 0 and s % ROW_TILE == 0 and s % RW_PREP_TILE == 0
    assert b % RW_BATCH_BLOCK == 0 and s % RW_CHUNK == 0
    c_uq = 2 * BRANCH_WIDTH
    c_uqkv = 4 * BRANCH_WIDTH
    c_zr = c_uqkv + RW_SHIFT_WIDTH

    strip = _bias_strip(rel_bias, s)
    xf, xb = _ln0(x.reshape(n, dm), ln0_g, ln0_b)
    for l in range(DEPTH):
        wl = w_in[l].astype(BF16)
        uq = _proj(xb, wl[:, :c_uq], c_uq, "proj_uq").reshape(b, s, c_uq)
        kv = _proj(xb, wl[:, c_uq:c_uqkv], c_uq, "proj_kv", BF16).reshape(b, s, c_uq)

        y_a = _pool(uq, pool_w[l], pool_scale[l])
        lam_init = 0.8 - 0.6 * math.exp(-0.3 * l)
        y_b = _diff_attention(uq, kv, strip, da_lambda[l], da_subln_g[l], lam_init)
        r, v, kk, lw, kd, be, gate, bonus = _rw_prep(
            xb.reshape(b, s, dm), wl[:, c_uqkv:c_zr], rw_mu_prev[l], rw_mu_next[l], rw_w0[l], rw_w2[l], rw_a0[l], rw_a2[l], rw_g2[l],
            rw_k_k[l], rw_k_a[l], rw_r_k[l])
        yf, yr = _rw_scan(r, v, kk, lw, kd, be)

        flat = lambda t: t.reshape(n, -1)
        xf = _merge(flat(y_a), flat(y_b), flat(yf), flat(yr), flat(gate), flat(bonus), rw_gn_g[l], rw_gn_b[l],
                    wl[:, c_zr:], w_branch[l], b_gate[l], w_out[l], xf, ln1_g[l], ln1_b[l])
        xf, xb = _mlp(xf, w_up[l], w_down[l], ln2_g[l], ln2_b[l])
    return xf.reshape(b, s, dm)
```

```python
import functools
import math

import numpy as np
import jax
import jax.numpy as jnp
from jax import lax
from jax.experimental import pallas as pl
from jax.experimental.pallas import tpu as pltpu

D_MODEL = 1024
DEPTH = 4
BRANCH_WIDTH = 512
POOL_WINDOWS = (2, 4, 8, 16)
POOL_GROUP_DIM = 128
DA_HEAD_DIM = 64
DA_HEADS = 4
REL_BUCKETS = 32
REL_MAX_DIST = 128
RW_WIDTH = 512
RW_HEAD_DIM = 64
RW_HEADS = 8
DECAY_RANK = 64
ICLR_RANK = 64
GATE_RANK = 128
RW_SHIFT_WIDTH = 3 * RW_WIDTH + 2 * DECAY_RANK + 2 * ICLR_RANK + GATE_RANK
GN_EPS = 64e-5
D_FF = 4 * D_MODEL
LN_EPS = 1e-5
ALPHA = (2.0 * DEPTH) ** 0.25
LOG2E = math.log2(math.e)

VMEM_LIMIT_BYTES = 56 * 1024 * 1024

ROW_TILE = 512
MLP_ROW_TILE = 512
MLP_FF_SLICE = 1024
ATTN_Q_TILE = 256
ATTN_HEADS_PER_STEP = 4
RW_PREP_TILE = 512
RW_HALO = 16
RW_CHUNK = 64
RW_CHUNKS_PER_STEP = 2
RW_BATCH_BLOCK = 2
RW_INV_BASE = 8
INV_PASSES = 1
MIX_PASSES = 1

F32 = jnp.float32
BF16 = jnp.bfloat16
NN = (((1,), (0,)), ((), ()))
NT = (((1,), (1,)), ((), ()))
TN = (((0,), (0,)), ((), ()))


def _cparams(*sem):
    return pltpu.CompilerParams(dimension_semantics=sem, vmem_limit_bytes=VMEM_LIMIT_BYTES)


def _layer_norm(x, g, b):
    mu = jnp.mean(x, axis=-1, keepdims=True)
    xc = x - mu
    var = jnp.mean(xc * xc, axis=-1, keepdims=True)
    return xc * lax.rsqrt(var + LN_EPS) * g + b


def _split_bf16(x, terms):
    parts = []
    for _ in range(terms):
        p = x.astype(BF16)
        parts.append(p)
        x = x - p.astype(F32)
    return parts


def _hilo_weights(b):
    hi, lo = _split_bf16(b, 2)
    return jnp.concatenate([hi, hi], axis=0), lo


def _mm_hilo(a, b_hihi, b_lo):
    ah, al = _split_bf16(a, 2)
    return (jnp.dot(jnp.concatenate([ah, al], axis=1), b_hihi, preferred_element_type=F32)
            + jnp.dot(ah, b_lo, preferred_element_type=F32))


def _mm(a, b, dims, passes):
    if passes == 1:
        return lax.dot_general(a.astype(BF16), b.astype(BF16), dims, preferred_element_type=F32)
    assert passes == 3 and dims == NN
    return _mm_hilo(a, *_hilo_weights(b))


def _ln0_kernel(x_ref, g_ref, b_ref, o_ref, ob_ref):
    y = _layer_norm(x_ref[...], g_ref[...], b_ref[...])
    o_ref[...] = y
    ob_ref[...] = y.astype(BF16)


def _ln0(x2, g, b):
    n = x2.shape[0]
    row = pl.BlockSpec((ROW_TILE, D_MODEL), lambda i: (i, 0))
    vec = pl.BlockSpec((1, D_MODEL), lambda i: (0, 0))
    return pl.pallas_call(
        _ln0_kernel,
        grid=(n // ROW_TILE,),
        in_specs=[row, vec, vec],
        out_specs=[row, row],
        out_shape=[jax.ShapeDtypeStruct((n, D_MODEL), F32), jax.ShapeDtypeStruct((n, D_MODEL), BF16)],
        compiler_params=_cparams("parallel"),
        name="ln0",
    )(x2, g.reshape(1, -1), b.reshape(1, -1))


def _proj_kernel(x_ref, w_ref, o_ref):
    o_ref[...] = jnp.dot(x_ref[...], w_ref[...], preferred_element_type=F32).astype(o_ref.dtype)


def _proj(xb, w, tn, name, out_dtype=F32):
    n, k = xb.shape
    m = w.shape[1]
    return pl.pallas_call(
        _proj_kernel,
        grid=(n // ROW_TILE, m // tn),
        in_specs=[pl.BlockSpec((ROW_TILE, k), lambda i, j: (i, 0)),
                  pl.BlockSpec((k, tn), lambda i, j: (0, j))],
        out_specs=pl.BlockSpec((ROW_TILE, tn), lambda i, j: (i, j)),
        out_shape=jax.ShapeDtypeStruct((n, m), out_dtype),
        compiler_params=_cparams("parallel", "arbitrary"),
        name=name,
    )(xb, w)


def _pool_kernel(u_ref, w_ref, s_ref, o_ref):
    s = u_ref.shape[1]
    t = lax.broadcasted_iota(jnp.int32, (s, POOL_GROUP_DIM), 0)
    for gi, win in enumerate(POOL_WINDOWS):
        left = win // 2
        right = win - 1 - left
        cols = slice(gi * POOL_GROUP_DIM, (gi + 1) * POOL_GROUP_DIM)
        x = u_ref[0, :, cols]
        acc = x
        for d in range(-left, right + 1):
            if d == 0:
                continue
            shifted = pltpu.roll(x, (-d) % s, axis=0)
            valid = jnp.logical_and(t + d >= 0, t + d < s)
            acc = acc + jnp.where(valid, shifted, 0.0)
        cnt = (jnp.minimum(t + right + 1, s) - jnp.maximum(t - left, 0)).astype(F32)
        mixed = acc / cnt - x
        y = jnp.dot(mixed.astype(BF16), w_ref[gi], preferred_element_type=F32)
        o_ref[0, :, cols] = (y * s_ref[:, cols]).astype(BF16)


def _pool(uq, pool_w, pool_scale):
    b, s, _ = uq.shape
    return pl.pallas_call(
        _pool_kernel,
        grid=(b,),
        in_specs=[pl.BlockSpec((1, s, BRANCH_WIDTH), lambda i: (i, 0, 0)),
                  pl.BlockSpec((len(POOL_WINDOWS), POOL_GROUP_DIM, POOL_GROUP_DIM), lambda i: (0, 0, 0)),
                  pl.BlockSpec((1, BRANCH_WIDTH), lambda i: (0, 0))],
        out_specs=pl.BlockSpec((1, s, BRANCH_WIDTH), lambda i: (i, 0, 0)),
        out_shape=jax.ShapeDtypeStruct((b, s, BRANCH_WIDTH), BF16),
        compiler_params=_cparams("parallel"),
        name="pool",
    )(uq, pool_w.astype(BF16), pool_scale.reshape(1, -1))


def _rel_bucket_np(rel):
    half = REL_BUCKETS // 2
    max_exact = half // 2
    n = np.abs(rel)
    nf = np.maximum(n, 1).astype(np.float64)
    large = max_exact + (np.log(nf / max_exact) / math.log(REL_MAX_DIST / max_exact)
                         * (half - max_exact)).astype(np.int32)
    large = np.minimum(large, half - 1)
    return (np.where(rel > 0, half, 0) + np.where(n < max_exact, n, large)).astype(np.int32)


def _bias_strip_kernel(tab_ref, bk_ref, o_ref):
    h = pl.program_id(0)
    bk = bk_ref[...]
    acc = jnp.zeros(bk.shape, F32)
    for j in range(REL_BUCKETS):
        acc = jnp.where(bk == j, tab_ref[j * DA_HEADS + h] * LOG2E, acc)
    o_ref[0] = acc


def _bias_strip(rel_bias, s):
    tq = ATTN_Q_TILE
    w = 2 * s - tq
    rel = np.arange(w)[None, :] - np.arange(tq)[:, None] - (s - tq)
    bucket = jnp.asarray(_rel_bucket_np(rel))
    return pl.pallas_call(
        _bias_strip_kernel,
        grid=(DA_HEADS,),
        in_specs=[pl.BlockSpec(memory_space=pltpu.SMEM),
                  pl.BlockSpec((tq, w), lambda h: (0, 0))],
        out_specs=pl.BlockSpec((1, tq, w), lambda h: (h, 0, 0)),
        out_shape=jax.ShapeDtypeStruct((DA_HEADS, tq, w), F32),
        compiler_params=_cparams("arbitrary"),
        name="bias_strip",
    )(rel_bias.reshape(-1), bucket)


def _attn_kernel(lam0_ref, lamv_ref, q_ref, k_ref, v_ref, strip_ref, g_ref, o_ref):
    tq = q_ref.shape[1]
    s = k_ref.shape[1]
    hw = 2 * DA_HEAD_DIM
    qi = pl.program_id(2)
    nq = pl.num_programs(2)
    lam_init = lam0_ref[0]
    lv = lamv_ref[...]
    lam = (jnp.exp(jnp.sum(lv[0:1] * lv[1:2], axis=-1, keepdims=True))
           - jnp.exp(jnp.sum(lv[2:3] * lv[3:4], axis=-1, keepdims=True)) + lam_init)
    off = pl.multiple_of((nq - 1 - qi) * tq, tq)
    lane = lax.broadcasted_iota(jnp.int32, (tq, hw), 1)

    def logits(h, first_half):
        cols = slice(h * hw, (h + 1) * hw)
        q = q_ref[0, :, cols] * (DA_HEAD_DIM ** -0.5 * LOG2E)
        qm = jnp.where((lane < DA_HEAD_DIM) == first_half, q, 0.0).astype(BF16)
        return (lax.dot_general(qm, k_ref[0, :, cols], NT, preferred_element_type=F32)
                + strip_ref[h, :, pl.ds(off, s)])

    def softmax_times_v(h, lg):
        m = jnp.max(lg, axis=-1, keepdims=True)
        e = jnp.exp2(lg - m)
        l = jnp.sum(e, axis=-1, keepdims=True)
        return jnp.dot(e.astype(BF16), v_ref[0, :, h * hw:(h + 1) * hw], preferred_element_type=F32) / l

    streams = [(h, first) for h in range(ATTN_HEADS_PER_STEP) for first in (True, False)]
    pending = logits(*streams[0])
    outs = []
    for idx, (h, _) in enumerate(streams):
        lg = pending
        if idx + 1 < len(streams):
            pending = logits(*streams[idx + 1])
        outs.append(softmax_times_v(h, lg))
    for h in range(ATTN_HEADS_PER_STEP):
        o = outs[2 * h] - lam * outs[2 * h + 1]
        o = o * lax.rsqrt(jnp.mean(o * o, axis=-1, keepdims=True) + 1e-5) * g_ref[...]
        o_ref[0, :, h * hw:(h + 1) * hw] = (o * (1.0 - lam_init)).astype(BF16)


def _diff_attention(uq, kv, strip, da_lambda, subln_g, lam_init):
    b, s, _ = uq.shape
    tq = ATTN_Q_TILE
    nh = ATTN_HEADS_PER_STEP
    hw = 2 * DA_HEAD_DIM
    bw = nh * hw
    q_blk0 = BRANCH_WIDTH // bw
    v_blk0 = BRANCH_WIDTH // bw
    return pl.pallas_call(
        _attn_kernel,
        grid=(b, DA_HEADS // nh, s // tq),
        in_specs=[pl.BlockSpec(memory_space=pltpu.SMEM),
                  pl.BlockSpec((4, DA_HEAD_DIM), lambda bi, hp, qi: (0, 0)),
                  pl.BlockSpec((1, tq, bw), lambda bi, hp, qi: (bi, qi, q_blk0 + hp)),
                  pl.BlockSpec((1, s, bw), lambda bi, hp, qi: (bi, 0, hp)),
                  pl.BlockSpec((1, s, bw), lambda bi, hp, qi: (bi, 0, v_blk0 + hp)),
                  pl.BlockSpec((nh, tq, strip.shape[2]), lambda bi, hp, qi: (hp, 0, 0),
                               pipeline_mode=pl.Buffered(1 if nh == DA_HEADS else 2)),
                  pl.BlockSpec((1, hw), lambda bi, hp, qi: (0, 0))],
        out_specs=pl.BlockSpec((1, tq, bw), lambda bi, hp, qi: (bi, qi, hp)),
        out_shape=jax.ShapeDtypeStruct((b, s, BRANCH_WIDTH), BF16),
        compiler_params=_cparams("parallel", "parallel", "arbitrary"),
        name="diff_attn",
    )(jnp.full((1,), lam_init, F32), da_lambda, uq, kv, kv, strip, subln_g.reshape(1, -1))


SEG_LANES = 256


def _head_segment_matrix():
    heads = SEG_LANES // RW_HEAD_DIM
    return jnp.asarray(np.kron(np.eye(heads), np.ones((RW_HEAD_DIM, RW_HEAD_DIM))), dtype=BF16)


def _seg_sum(x, seg):
    parts = _split_bf16(x, 2)
    blocks = []
    for c in range(x.shape[1] // SEG_LANES):
        cols = slice(c * SEG_LANES, (c + 1) * SEG_LANES)
        blocks.append(sum(jnp.dot(p[:, cols], seg, preferred_element_type=F32) for p in parts))
    return jnp.concatenate(blocks, axis=1)


def _sigmoid(x):
    return 1.0 / (1.0 + jnp.exp(-x))


def _rw_prep_kernel(x_ref, xp_ref, xn_ref, wz_ref, mup_ref, mun_ref, w0_ref, w2h_ref, w2l_ref, a0_ref, a2h_ref,
                    a2l_ref, g2h_ref, g2l_ref, kk_w_ref, ka_w_ref, rk_w_ref, seg_ref,
                    r_ref, v_ref, kk_ref, lw_ref, kd_ref, be_ref, gate_ref, bonus_ref):
    i = pl.program_id(1)
    n = pl.num_programs(1)
    ts = x_ref.shape[1]
    halo = RW_HALO
    xe = jnp.concatenate([xp_ref[0], x_ref[0], xn_ref[0]], axis=0)
    ze = jnp.dot(xe, wz_ref[...], preferred_element_type=F32)
    z = ze[halo:halo + ts]
    slab = 8
    t = lax.broadcasted_iota(jnp.int32, (slab, 1), 0)
    prev_row = jnp.where(i > 0, ze[halo - 1:halo], 0.0)
    next_row = jnp.where(i < n - 1, ze[halo + ts:halo + ts + 1], 0.0)
    prev = pltpu.roll(z, 1, axis=0)
    prev = jnp.concatenate([jnp.where(t == 0, prev_row, prev[:slab]), prev[slab:]], axis=0)
    nxt = pltpu.roll(z, ts - 1, axis=0)
    nxt = jnp.concatenate([nxt[:ts - slab], jnp.where(t == slab - 1, next_row, nxt[ts - slab:])], axis=0)
    z = z + mup_ref[...] * (prev - z) + mun_ref[...] * (nxt - z)

    w = RW_WIDTH
    r = z[:, 0:w]
    k = z[:, w:2 * w]
    v = z[:, 2 * w:3 * w]
    lw = z[:, 3 * w:3 * w + 2 * DECAY_RANK]
    la = z[:, 3 * w + 2 * DECAY_RANK:3 * w + 2 * DECAY_RANK + 2 * ICLR_RANK]
    lg = z[:, 3 * w + 2 * DECAY_RANK + 2 * ICLR_RANK:]

    x = w0_ref[...] + _mm_hilo(jnp.tanh(lw), w2h_ref[...], w2l_ref[...])
    log_decay = -math.exp(-0.5) / (1.0 + jnp.exp(-x))
    iclr = _sigmoid(a0_ref[...] + _mm_hilo(la, a2h_ref[...], a2l_ref[...]))
    gate = _mm_hilo(_sigmoid(lg), g2h_ref[...], g2l_ref[...])

    seg = seg_ref[...]
    kk = k * kk_w_ref[...]
    kk = kk * lax.rsqrt(jnp.maximum(_seg_sum(kk * kk, seg), 1e-24))
    bonus = _seg_sum(r * k * rk_w_ref[...], seg) * v

    r_ref[0] = r
    v_ref[0] = v
    kk_ref[0] = kk
    lw_ref[0] = log_decay
    gate_ref[0] = gate
    bonus_ref[0] = bonus
    ka = ka_w_ref[...]
    for d in range(2):
        a_d = iclr[:, d * w:(d + 1) * w]
        kd_ref[0, :, d * w:(d + 1) * w] = k * (1.0 + (a_d - 1.0) * ka)
        be_ref[0, :, d * w:(d + 1) * w] = kk * a_d


def _block_diag2(m):
    z = jnp.zeros_like(m[0])
    return jnp.concatenate([jnp.concatenate([m[0], z], axis=1), jnp.concatenate([z, m[1]], axis=1)], axis=0)


def _rw_prep(xb, w_zr, mu_prev, mu_next, w0, w2, a0, a2, g2, k_k, k_a, r_k):
    b, s, dm = xb.shape
    c = w_zr.shape[1]
    ts = RW_PREP_TILE
    w = RW_WIDTH
    seg = _head_segment_matrix()
    halo = RW_HALO
    nhalo = s // halo

    def vec(width):
        return pl.BlockSpec((1, width), lambda bi, i: (0, 0))

    def mat(rows, cols):
        return pl.BlockSpec((rows, cols), lambda bi, i: (0, 0))

    def out(width):
        return pl.BlockSpec((1, ts, width), lambda bi, i: (bi, i, 0))

    shapes = [jax.ShapeDtypeStruct((b, s, width), F32) for width in (w, w, w, 2 * w, 2 * w, 2 * w, w, w)]
    return pl.pallas_call(
        _rw_prep_kernel,
        grid=(b, s // ts),
        in_specs=[pl.BlockSpec((1, ts, dm), lambda bi, i: (bi, i, 0)),
                  pl.BlockSpec((1, halo, dm), lambda bi, i: (bi, jnp.maximum(i * (ts // halo) - 1, 0), 0)),
                  pl.BlockSpec((1, halo, dm), lambda bi, i: (bi, jnp.minimum((i + 1) * (ts // halo), nhalo - 1), 0)),
                  mat(dm, c), vec(c), vec(c), vec(2 * w), mat(4 * DECAY_RANK, 2 * w), mat(2 * DECAY_RANK, 2 * w),
                  vec(2 * w), mat(4 * ICLR_RANK, 2 * w), mat(2 * ICLR_RANK, 2 * w),
                  mat(2 * GATE_RANK, w), mat(GATE_RANK, w), vec(w), vec(w), vec(w), mat(SEG_LANES, SEG_LANES)],
        out_specs=[out(w), out(w), out(w), out(2 * w), out(2 * w), out(2 * w), out(w), out(w)],
        out_shape=shapes,
        compiler_params=_cparams("parallel", "arbitrary"),
        name="rw_prep",
    )(xb, xb, xb, w_zr, mu_prev.reshape(1, -1), mu_next.reshape(1, -1), w0.reshape(1, -1),
      *_hilo_weights(_block_diag2(w2)),
      a0.reshape(1, -1), *_hilo_weights(_block_diag2(a2)), *_hilo_weights(g2),
      k_k.reshape(1, -1), k_a.reshape(1, -1), r_k.reshape(1, -1), seg)


def _unit_triangular_inverses(mats):
    n = mats[0].shape[0]
    ti = lax.broadcasted_iota(jnp.int32, (n, n), 0)
    si = lax.broadcasted_iota(jnp.int32, (n, n), 1)

    def same_block(m):
        return (ti // m) == (si // m)

    mm = functools.partial(_mm, dims=NN, passes=INV_PASSES)
    eye = (ti == si).astype(F32)
    base = same_block(RW_INV_BASE)
    diag = [jnp.where(base, a, 0.0) for a in mats]
    power = [mm(a, a) for a in diag]
    inv = [mm(eye + a, eye + p) for a, p in zip(diag, power)]
    m = 4
    while m < RW_INV_BASE:
        power = [mm(p, p) for p in power]
        inv = [mm(t, eye + p) for t, p in zip(inv, power)]
        m *= 2
    m = RW_INV_BASE
    while m < n:
        ring = jnp.logical_and(same_block(2 * m), jnp.logical_not(same_block(m)))
        tmp = [mm(jnp.where(ring, a, 0.0), t) for a, t in zip(mats, inv)]
        inv = [t + mm(t, x) for t, x in zip(inv, tmp)]
        m *= 2
    return inv


def _rw_scan_kernel(rf_ref, vf_ref, kkf_ref, lwf_ref, kdf_ref, bef_ref,
                    rb_ref, vb_ref, kkb_ref, lwb_ref, kdb_ref, beb_ref, yf_ref, yb_ref, state_ref):
    @pl.when(pl.program_id(1) == 0)
    def _():
        state_ref[...] = jnp.zeros_like(state_ref)

    n = RW_CHUNK
    hd = RW_HEAD_DIM
    ti = lax.broadcasted_iota(jnp.int32, (2 * n, 2 * n), 0)
    si = lax.broadcasted_iota(jnp.int32, (2 * n, 2 * n), 1)

    items = []
    for d, (r_ref, v_ref, kk_ref, lw_ref, kd_ref, be_ref) in enumerate((
            (rf_ref, vf_ref, kkf_ref, lwf_ref, kdf_ref, bef_ref),
            (rb_ref, vb_ref, kkb_ref, lwb_ref, kdb_ref, beb_ref))):
        lag = (ti % n - si % n) * (1 - 2 * d)
        keep = lag >= jnp.where(ti < n, 1, 0)
        tri = (lag[:n, :n] >= 0).astype(BF16)
        for order in range(RW_CHUNKS_PER_STEP):
            pos = order if d == 0 else RW_CHUNKS_PER_STEP - 1 - order
            rows = slice(pos * n, (pos + 1) * n)
            for bb in range(RW_BATCH_BLOCK):
                lw = lw_ref[bb, rows, :]
                cum = sum(jnp.dot(tri, part, preferred_element_type=F32) for part in _split_bf16(lw, 3))
                e_neg = jnp.exp(-cum)
                ar_all = jnp.concatenate([-kk_ref[bb, rows, :] * jnp.exp(cum - lw),
                                          r_ref[bb, rows, :] * jnp.exp(cum)], axis=0)
                bk_all = jnp.concatenate([be_ref[bb, rows, :] * e_neg, kd_ref[bb, rows, :] * e_neg], axis=0)
                v_all = v_ref[bb, rows, :]
                e_tot = jnp.exp(jnp.sum(lw, axis=0, keepdims=True))
                for h in range(RW_HEADS):
                    cols = slice(h * hd, (h + 1) * hd)
                    items.append(dict(order=order, chain=(bb, d, h), rows=rows, cols=cols, keep=keep,
                                      ar=ar_all[:, cols], bk=bk_all[:, cols], v=v_all[:, cols], decay=e_tot[:, cols]))

    for it in items:
        it["m"] = jnp.where(it["keep"], _mm(it["ar"], it["bk"], NT, MIX_PASSES), 0.0)
    for it, inv in zip(items, _unit_triangular_inverses([it["m"][:n, :n] for it in items])):
        it["inv"] = inv

    state = {}
    for bb in range(RW_BATCH_BLOCK):
        for d in range(2):
            for h in range(RW_HEADS):
                state[(bb, d, h)] = state_ref[bb, d, h]
    for order in range(RW_CHUNKS_PER_STEP):
        group = [it for it in items if it["order"] == order]
        for it in group:
            it["xy"] = (_mm(it["ar"], state[it["chain"]], NT, MIX_PASSES)
                        + _mm(it["m"][:, n:], it["v"], NN, MIX_PASSES))
        for it in group:
            it["u"] = _mm(it["inv"], it["xy"][:n], NN, INV_PASSES)
        for it in group:
            bb, d, _ = it["chain"]
            y = it["xy"][n:] + _mm(it["m"][n:, :n], it["u"], NN, MIX_PASSES)
            (yf_ref, yb_ref)[d][bb, it["rows"], it["cols"]] = y
        for it in group:
            upd = _mm(jnp.concatenate([it["u"], it["v"]], axis=0), it["bk"], TN, MIX_PASSES)
            state[it["chain"]] = (state[it["chain"]] + upd) * it["decay"]
    for (bb, d, h), st in state.items():
        state_ref[bb, d, h] = st


def _rw_scan(r, v, kk, lw, kd, be):
    b, s, w = r.shape
    rows = RW_CHUNK * RW_CHUNKS_PER_STEP
    nblk = s // rows
    nb = RW_BATCH_BLOCK

    def spec(d, col):
        if d == 0:
            return pl.BlockSpec((nb, rows, w), lambda bi, c: (bi, c, col))
        return pl.BlockSpec((nb, rows, w), lambda bi, c: (bi, nblk - 1 - c, col))

    in_specs = []
    for d in range(2):
        in_specs += [spec(d, 0), spec(d, 0), spec(d, 0), spec(d, d), spec(d, d), spec(d, d)]
    return pl.pallas_call(
        _rw_scan_kernel,
        grid=(b // nb, nblk),
        in_specs=in_specs,
        out_specs=[spec(0, 0), spec(1, 0)],
        out_shape=[jax.ShapeDtypeStruct((b, s, w), F32)] * 2,
        scratch_shapes=[pltpu.VMEM((nb, 2, RW_HEADS, RW_HEAD_DIM, RW_HEAD_DIM), F32)],
        compiler_params=_cparams("parallel", "arbitrary"),
        name="rw_scan",
    )(r, v, kk, lw, kd, be, r, v, kk, lw, kd, be)


def _rw_finish(yf, yb, gate, bonus, gn_g, gn_b, seg):
    y = yf + yb
    mu = _seg_sum(y, seg) * (1.0 / RW_HEAD_DIM)
    yc = y - mu
    var = _seg_sum(yc * yc, seg) * (1.0 / RW_HEAD_DIM)
    yn = yc * lax.rsqrt(var + GN_EPS) * gn_g + gn_b
    return ((yn + bonus) * gate).astype(BF16)


def _merge_kernel(ya_ref, yb_ref, yf_ref, yr_ref, rgate_ref, bonus_ref, gn_g_ref, gn_b_ref, seg_ref,
                  wg_ref, wb_ref, bg_ref, wo_ref, x_ref, g_ref, b_ref, o_ref):
    x = x_ref[...]
    xb = x.astype(BF16)
    yc = _rw_finish(yf_ref[...], yr_ref[...], rgate_ref[...], bonus_ref[...], gn_g_ref[...], gn_b_ref[...],
                    seg_ref[...])
    merged = None
    for n, y in enumerate((ya_ref[...], yb_ref[...], yc)):
        cols = slice(n * D_MODEL, (n + 1) * D_MODEL)
        gate_logits = jnp.dot(xb, wg_ref[:, cols], preferred_element_type=F32) + bg_ref[:, cols]
        branch = jnp.dot(y, wb_ref[n], preferred_element_type=F32)
        term = _sigmoid(gate_logits) * branch
        merged = term if merged is None else merged + term
    mix = jnp.dot(merged.astype(BF16), wo_ref[...], preferred_element_type=F32)
    o_ref[...] = _layer_norm(ALPHA * x + mix, g_ref[...], b_ref[...])


def _merge(ya, yb, yf, yr, rgate, bonus, gn_g, gn_b, w_gate, w_branch, b_gate, w_out, x, g, b):
    n = x.shape[0]
    tm = ROW_TILE
    ytile = pl.BlockSpec((tm, BRANCH_WIDTH), lambda i: (i, 0))
    row = pl.BlockSpec((tm, D_MODEL), lambda i: (i, 0))
    vec = pl.BlockSpec((1, D_MODEL), lambda i: (0, 0))
    hvec = pl.BlockSpec((1, RW_WIDTH), lambda i: (0, 0))
    return pl.pallas_call(
        _merge_kernel,
        grid=(n // tm,),
        in_specs=[ytile, ytile, ytile, ytile, ytile, ytile, hvec, hvec,
                  pl.BlockSpec((SEG_LANES, SEG_LANES), lambda i: (0, 0)),
                  pl.BlockSpec((D_MODEL, 3 * D_MODEL), lambda i: (0, 0)),
                  pl.BlockSpec((3, BRANCH_WIDTH, D_MODEL), lambda i: (0, 0, 0)),
                  pl.BlockSpec((1, 3 * D_MODEL), lambda i: (0, 0)),
                  pl.BlockSpec((D_MODEL, D_MODEL), lambda i: (0, 0)),
                  row, vec, vec],
        out_specs=row,
        out_shape=jax.ShapeDtypeStruct((n, D_MODEL), F32),
        compiler_params=_cparams("parallel"),
        name="merge",
    )(ya, yb, yf, yr, rgate, bonus, gn_g.reshape(1, -1), gn_b.reshape(1, -1), _head_segment_matrix(),
      w_gate, w_branch.astype(BF16), b_gate.reshape(1, -1), w_out.astype(BF16), x, g.reshape(1, -1), b.reshape(1, -1))


def _mlp_kernel(x_ref, wu_ref, wd_ref, g_ref, b_ref, o_ref, ob_ref):
    x = x_ref[...]
    xb = x.astype(BF16)
    acc = None
    for j in range(D_FF // MLP_FF_SLICE):
        ff = slice(j * MLP_FF_SLICE, (j + 1) * MLP_FF_SLICE)
        hid = jnp.maximum(jnp.dot(xb, wu_ref[:, ff], preferred_element_type=F32), 0.0)
        part = jnp.dot((hid * hid).astype(BF16), wd_ref[ff, :], preferred_element_type=F32)
        acc = part if acc is None else acc + part
    y = _layer_norm(ALPHA * x + acc, g_ref[...], b_ref[...])
    o_ref[...] = y
    ob_ref[...] = y.astype(BF16)


def _mlp(x, w_up, w_down, g, b):
    n = x.shape[0]
    tm = MLP_ROW_TILE
    row = pl.BlockSpec((tm, D_MODEL), lambda i: (i, 0))
    vec = pl.BlockSpec((1, D_MODEL), lambda i: (0, 0))
    resident = dict(pipeline_mode=pl.Buffered(1))
    return pl.pallas_call(
        _mlp_kernel,
        grid=(n // tm,),
        in_specs=[row,
                  pl.BlockSpec((D_MODEL, D_FF), lambda i: (0, 0), **resident),
                  pl.BlockSpec((D_FF, D_MODEL), lambda i: (0, 0), **resident),
                  vec, vec],
        out_specs=[row, row],
        out_shape=[jax.ShapeDtypeStruct((n, D_MODEL), F32), jax.ShapeDtypeStruct((n, D_MODEL), BF16)],
        compiler_params=_cparams("parallel"),
        name="mlp",
    )(x, w_up.astype(BF16), w_down.astype(BF16), g.reshape(1, -1), b.reshape(1, -1))


def kernel(x, ln0_g, ln0_b, w_in, pool_w, pool_scale, da_lambda, da_subln_g, rel_bias, rw_mu_prev, rw_mu_next, rw_w0, rw_w2, rw_a0, rw_a2, rw_g2, rw_k_k, rw_k_a, rw_r_k, rw_gn_g, rw_gn_b, w_branch, b_gate, w_out, ln1_g, ln1_b, w_up, w_down, ln2_g, ln2_b):
    b, s, dm = x.shape
    n = b * s
    assert dm == D_MODEL and n % MLP_ROW_TILE == 0 and n % ROW_TILE == 0
    assert s % ATTN_Q_TILE == 0 and s % ROW_TILE == 0 and s % RW_PREP_TILE == 0
    assert b % RW_BATCH_BLOCK == 0 and s % (RW_CHUNK * RW_CHUNKS_PER_STEP) == 0
    c_uq = 2 * BRANCH_WIDTH
    c_uqkv = 4 * BRANCH_WIDTH
    c_zr = c_uqkv + RW_SHIFT_WIDTH

    strip = _bias_strip(rel_bias, s)
    xf, xb = _ln0(x.reshape(n, dm), ln0_g, ln0_b)
    for l in range(DEPTH):
        wl = w_in[l].astype(BF16)
        uq = _proj(xb, wl[:, :c_uq], c_uq, "proj_uq").reshape(b, s, c_uq)
        kv = _proj(xb, wl[:, c_uq:c_uqkv], c_uq, "proj_kv", BF16).reshape(b, s, c_uq)

        y_a = _pool(uq, pool_w[l], pool_scale[l])
        lam_init = 0.8 - 0.6 * math.exp(-0.3 * l)
        y_b = _diff_attention(uq, kv, strip, da_lambda[l], da_subln_g[l], lam_init)
        r, v, kk, lw, kd, be, gate, bonus = _rw_prep(
            xb.reshape(b, s, dm), wl[:, c_uqkv:c_zr], rw_mu_prev[l], rw_mu_next[l], rw_w0[l], rw_w2[l], rw_a0[l], rw_a2[l], rw_g2[l],
            rw_k_k[l], rw_k_a[l], rw_r_k[l])
        yf, yr = _rw_scan(r, v, kk, lw, kd, be)

        flat = lambda t: t.reshape(n, -1)
        xf = _merge(flat(y_a), flat(y_b), flat(yf), flat(yr), flat(gate), flat(bonus), rw_gn_g[l], rw_gn_b[l],
                    wl[:, c_zr:], w_branch[l], b_gate[l], w_out[l], xf, ln1_g[l], ln1_b[l])
        xf, xb = _mlp(xf, w_up[l], w_down[l], ln2_g[l], ln2_b[l])
    return xf.reshape(b, s, dm)
```

```python
import functools
import math

import numpy as np
import jax
import jax.numpy as jnp
from jax import lax
from jax.experimental import pallas as pl
from jax.experimental.pallas import tpu as pltpu

D_MODEL = 1024
DEPTH = 4
BRANCH_WIDTH = 512
POOL_WINDOWS = (2, 4, 8, 16)
POOL_GROUP_DIM = 128
DA_HEAD_DIM = 64
DA_HEADS = 4
REL_BUCKETS = 32
REL_MAX_DIST = 128
RW_WIDTH = 512
RW_HEAD_DIM = 64
RW_HEADS = 8
DECAY_RANK = 64
ICLR_RANK = 64
GATE_RANK = 128
RW_SHIFT_WIDTH = 3 * RW_WIDTH + 2 * DECAY_RANK + 2 * ICLR_RANK + GATE_RANK
GN_EPS = 64e-5
D_FF = 4 * D_MODEL
LN_EPS = 1e-5
ALPHA = (2.0 * DEPTH) ** 0.25
LOG2E = math.log2(math.e)

VMEM_LIMIT_BYTES = 56 * 1024 * 1024

ROW_TILE = 512
MLP_ROW_TILE = 512
MLP_FF_SLICE = 1024
ATTN_Q_TILE = 256
ATTN_HEADS_PER_STEP = 4
RW_PREP_TILE = 512
RW_HALO = 16
RW_CHUNK = 64
RW_CHUNKS_PER_STEP = 2
RW_BATCH_BLOCK = 2
RW_INV_BASE = 8
INV_PASSES = 1
MIX_PASSES = 1

F32 = jnp.float32
BF16 = jnp.bfloat16
NN = (((1,), (0,)), ((), ()))
NT = (((1,), (1,)), ((), ()))
TN = (((0,), (0,)), ((), ()))


def _cparams(*sem):
    return pltpu.CompilerParams(dimension_semantics=sem, vmem_limit_bytes=VMEM_LIMIT_BYTES)


def _layer_norm(x, g, b):
    mu = jnp.mean(x, axis=-1, keepdims=True)
    xc = x - mu
    var = jnp.mean(xc * xc, axis=-1, keepdims=True)
    return xc * lax.rsqrt(var + LN_EPS) * g + b


def _split_bf16(x, terms):
    parts = []
    for _ in range(terms):
        p = x.astype(BF16)
        parts.append(p)
        x = x - p.astype(F32)
    return parts


def _hilo_weights(b):
    hi, lo = _split_bf16(b, 2)
    return jnp.concatenate([hi, hi], axis=0), lo


def _mm_hilo(a, b_hihi, b_lo):
    ah, al = _split_bf16(a, 2)
    return (jnp.dot(jnp.concatenate([ah, al], axis=1), b_hihi, preferred_element_type=F32)
            + jnp.dot(ah, b_lo, preferred_element_type=F32))


def _mm(a, b, dims, passes):
    if passes == 1:
        return lax.dot_general(a.astype(BF16), b.astype(BF16), dims, preferred_element_type=F32)
    assert passes == 3 and dims == NN
    return _mm_hilo(a, *_hilo_weights(b))


def _ln0_kernel(x_ref, g_ref, b_ref, o_ref, ob_ref):
    y = _layer_norm(x_ref[...], g_ref[...], b_ref[...])
    o_ref[...] = y
    ob_ref[...] = y.astype(BF16)


def _ln0(x2, g, b):
    n = x2.shape[0]
    row = pl.BlockSpec((ROW_TILE, D_MODEL), lambda i: (i, 0))
    vec = pl.BlockSpec((1, D_MODEL), lambda i: (0, 0))
    return pl.pallas_call(
        _ln0_kernel,
        grid=(n // ROW_TILE,),
        in_specs=[row, vec, vec],
        out_specs=[row, row],
        out_shape=[jax.ShapeDtypeStruct((n, D_MODEL), F32), jax.ShapeDtypeStruct((n, D_MODEL), BF16)],
        compiler_params=_cparams("parallel"),
        name="ln0",
    )(x2, g.reshape(1, -1), b.reshape(1, -1))


def _proj_kernel(x_ref, w_ref, u_ref, q_ref, kv_ref):
    x = x_ref[...]
    w = BRANCH_WIDTH
    u_ref[...] = jnp.dot(x, w_ref[:, :w], preferred_element_type=F32)
    q = jnp.dot(x, w_ref[:, w:2 * w], preferred_element_type=F32)
    q_ref[...] = (q * (DA_HEAD_DIM ** -0.5 * LOG2E)).astype(BF16)
    kv_ref[...] = jnp.dot(x, w_ref[:, 2 * w:], preferred_element_type=F32).astype(BF16)


def _proj(xb, w_uqkv):
    n, k = xb.shape
    w = BRANCH_WIDTH

    def out(width):
        return pl.BlockSpec((ROW_TILE, width), lambda i: (i, 0))

    return pl.pallas_call(
        _proj_kernel,
        grid=(n // ROW_TILE,),
        in_specs=[pl.BlockSpec((ROW_TILE, k), lambda i: (i, 0)),
                  pl.BlockSpec((k, 4 * w), lambda i: (0, 0))],
        out_specs=[out(w), out(w), out(2 * w)],
        out_shape=[jax.ShapeDtypeStruct((n, w), F32), jax.ShapeDtypeStruct((n, w), BF16),
                   jax.ShapeDtypeStruct((n, 2 * w), BF16)],
        compiler_params=_cparams("parallel"),
        name="proj_uqkv",
    )(xb, w_uqkv)


def _pool_kernel(u_ref, w_ref, s_ref, o_ref):
    s = u_ref.shape[1]
    t = lax.broadcasted_iota(jnp.int32, (s, POOL_GROUP_DIM), 0)
    for gi, win in enumerate(POOL_WINDOWS):
        left = win // 2
        right = win - 1 - left
        cols = slice(gi * POOL_GROUP_DIM, (gi + 1) * POOL_GROUP_DIM)
        x = u_ref[0, :, cols]
        acc = x
        for d in range(-left, right + 1):
            if d == 0:
                continue
            shifted = pltpu.roll(x, (-d) % s, axis=0)
            valid = jnp.logical_and(t + d >= 0, t + d < s)
            acc = acc + jnp.where(valid, shifted, 0.0)
        cnt = (jnp.minimum(t + right + 1, s) - jnp.maximum(t - left, 0)).astype(F32)
        mixed = acc / cnt - x
        y = jnp.dot(mixed.astype(BF16), w_ref[gi], preferred_element_type=F32)
        o_ref[0, :, cols] = (y * s_ref[:, cols]).astype(BF16)


def _pool(u, pool_w, pool_scale):
    b, s, _ = u.shape
    return pl.pallas_call(
        _pool_kernel,
        grid=(b,),
        in_specs=[pl.BlockSpec((1, s, BRANCH_WIDTH), lambda i: (i, 0, 0)),
                  pl.BlockSpec((len(POOL_WINDOWS), POOL_GROUP_DIM, POOL_GROUP_DIM), lambda i: (0, 0, 0)),
                  pl.BlockSpec((1, BRANCH_WIDTH), lambda i: (0, 0))],
        out_specs=pl.BlockSpec((1, s, BRANCH_WIDTH), lambda i: (i, 0, 0)),
        out_shape=jax.ShapeDtypeStruct((b, s, BRANCH_WIDTH), BF16),
        compiler_params=_cparams("parallel"),
        name="pool",
    )(u, pool_w.astype(BF16), pool_scale.reshape(1, -1))


def _rel_bucket_np(rel):
    half = REL_BUCKETS // 2
    max_exact = half // 2
    n = np.abs(rel)
    nf = np.maximum(n, 1).astype(np.float64)
    large = max_exact + (np.log(nf / max_exact) / math.log(REL_MAX_DIST / max_exact)
                         * (half - max_exact)).astype(np.int32)
    large = np.minimum(large, half - 1)
    return (np.where(rel > 0, half, 0) + np.where(n < max_exact, n, large)).astype(np.int32)


def _bias_strip_kernel(tab_ref, bk_ref, o_ref):
    h = pl.program_id(0)
    bk = bk_ref[...]
    acc = jnp.zeros(bk.shape, F32)
    for j in range(REL_BUCKETS):
        acc = jnp.where(bk == j, tab_ref[j * DA_HEADS + h] * LOG2E, acc)
    o_ref[0] = acc


def _bias_strip(rel_bias, s):
    tq = ATTN_Q_TILE
    w = 2 * s - tq
    rel = np.arange(w)[None, :] - np.arange(tq)[:, None] - (s - tq)
    bucket = jnp.asarray(_rel_bucket_np(rel))
    return pl.pallas_call(
        _bias_strip_kernel,
        grid=(DA_HEADS,),
        in_specs=[pl.BlockSpec(memory_space=pltpu.SMEM),
                  pl.BlockSpec((tq, w), lambda h: (0, 0))],
        out_specs=pl.BlockSpec((1, tq, w), lambda h: (h, 0, 0)),
        out_shape=jax.ShapeDtypeStruct((DA_HEADS, tq, w), F32),
        compiler_params=_cparams("arbitrary"),
        name="bias_strip",
    )(rel_bias.reshape(-1), bucket)


def _attn_kernel(lam0_ref, lamv_ref, q_ref, k_ref, v_ref, strip_ref, g_ref, o_ref):
    tq = q_ref.shape[1]
    s = k_ref.shape[1]
    hw = 2 * DA_HEAD_DIM
    qi = pl.program_id(2)
    nq = pl.num_programs(2)
    lam_init = lam0_ref[0]
    lv = lamv_ref[...]
    lam = (jnp.exp(jnp.sum(lv[0:1] * lv[1:2], axis=-1, keepdims=True))
           - jnp.exp(jnp.sum(lv[2:3] * lv[3:4], axis=-1, keepdims=True)) + lam_init)
    off = pl.multiple_of((nq - 1 - qi) * tq, tq)
    lane = lax.broadcasted_iota(jnp.int32, (tq, hw), 1)

    def logits(h, first_half):
        cols = slice(h * hw, (h + 1) * hw)
        q = q_ref[0, :, cols]
        qm = jnp.where((lane < DA_HEAD_DIM) == first_half, q, jnp.zeros_like(q))
        return (lax.dot_general(qm, k_ref[0, :, cols], NT, preferred_element_type=F32)
                + strip_ref[h, :, pl.ds(off, s)])

    def softmax_times_v(h, lg):
        m = jnp.max(lg, axis=-1, keepdims=True)
        e = jnp.exp2(lg - m)
        l = jnp.sum(e, axis=-1, keepdims=True)
        return jnp.dot(e.astype(BF16), v_ref[0, :, h * hw:(h + 1) * hw], preferred_element_type=F32) / l

    streams = [(h, first) for h in range(ATTN_HEADS_PER_STEP) for first in (True, False)]
    pending = logits(*streams[0])
    outs = []
    for idx, (h, _) in enumerate(streams):
        lg = pending
        if idx + 1 < len(streams):
            pending = logits(*streams[idx + 1])
        outs.append(softmax_times_v(h, lg))
    for h in range(ATTN_HEADS_PER_STEP):
        o = outs[2 * h] - lam * outs[2 * h + 1]
        o = o * lax.rsqrt(jnp.mean(o * o, axis=-1, keepdims=True) + 1e-5) * g_ref[...]
        o_ref[0, :, h * hw:(h + 1) * hw] = (o * (1.0 - lam_init)).astype(BF16)


def _diff_attention(q, kv, strip, da_lambda, subln_g, lam_init):
    b, s, _ = q.shape
    tq = ATTN_Q_TILE
    nh = ATTN_HEADS_PER_STEP
    hw = 2 * DA_HEAD_DIM
    bw = nh * hw
    v_blk0 = BRANCH_WIDTH // bw
    return pl.pallas_call(
        _attn_kernel,
        grid=(b, DA_HEADS // nh, s // tq),
        in_specs=[pl.BlockSpec(memory_space=pltpu.SMEM),
                  pl.BlockSpec((4, DA_HEAD_DIM), lambda bi, hp, qi: (0, 0)),
                  pl.BlockSpec((1, tq, bw), lambda bi, hp, qi: (bi, qi, hp)),
                  pl.BlockSpec((1, s, bw), lambda bi, hp, qi: (bi, 0, hp)),
                  pl.BlockSpec((1, s, bw), lambda bi, hp, qi: (bi, 0, v_blk0 + hp)),
                  pl.BlockSpec((nh, tq, strip.shape[2]), lambda bi, hp, qi: (hp, 0, 0),
                               pipeline_mode=pl.Buffered(1 if nh == DA_HEADS else 2)),
                  pl.BlockSpec((1, hw), lambda bi, hp, qi: (0, 0))],
        out_specs=pl.BlockSpec((1, tq, bw), lambda bi, hp, qi: (bi, qi, hp)),
        out_shape=jax.ShapeDtypeStruct((b, s, BRANCH_WIDTH), BF16),
        compiler_params=_cparams("parallel", "parallel", "arbitrary"),
        name="diff_attn",
    )(jnp.full((1,), lam_init, F32), da_lambda, q, kv, kv, strip, subln_g.reshape(1, -1))


SEG_LANES = 256


def _head_segment_matrix():
    heads = SEG_LANES // RW_HEAD_DIM
    return jnp.asarray(np.kron(np.eye(heads), np.ones((RW_HEAD_DIM, RW_HEAD_DIM))), dtype=BF16)


def _seg_sum(x, seg):
    parts = _split_bf16(x, 2)
    blocks = []
    for c in range(x.shape[1] // SEG_LANES):
        cols = slice(c * SEG_LANES, (c + 1) * SEG_LANES)
        blocks.append(sum(jnp.dot(p[:, cols], seg, preferred_element_type=F32) for p in parts))
    return jnp.concatenate(blocks, axis=1)


def _sigmoid(x):
    return 1.0 / (1.0 + jnp.exp(-x))


def _rw_prep_kernel(x_ref, xp_ref, xn_ref, wz_ref, mup_ref, mun_ref, w0_ref, w2h_ref, w2l_ref, a0_ref, a2h_ref,
                    a2l_ref, g2h_ref, g2l_ref, kk_w_ref, ka_w_ref, rk_w_ref, seg_ref,
                    r_ref, v_ref, kk_ref, lw_ref, kd_ref, be_ref, gate_ref, bonus_ref):
    i = pl.program_id(1)
    n = pl.num_programs(1)
    ts = x_ref.shape[1]
    halo = RW_HALO
    xe = jnp.concatenate([xp_ref[0], x_ref[0], xn_ref[0]], axis=0)
    ze = jnp.dot(xe, wz_ref[...], preferred_element_type=F32)
    z = ze[halo:halo + ts]
    slab = 8
    t = lax.broadcasted_iota(jnp.int32, (slab, 1), 0)
    prev_row = jnp.where(i > 0, ze[halo - 1:halo], 0.0)
    next_row = jnp.where(i < n - 1, ze[halo + ts:halo + ts + 1], 0.0)
    prev = pltpu.roll(z, 1, axis=0)
    prev = jnp.concatenate([jnp.where(t == 0, prev_row, prev[:slab]), prev[slab:]], axis=0)
    nxt = pltpu.roll(z, ts - 1, axis=0)
    nxt = jnp.concatenate([nxt[:ts - slab], jnp.where(t == slab - 1, next_row, nxt[ts - slab:])], axis=0)
    z = z + mup_ref[...] * (prev - z) + mun_ref[...] * (nxt - z)

    w = RW_WIDTH
    r = z[:, 0:w]
    k = z[:, w:2 * w]
    v = z[:, 2 * w:3 * w]
    lw = z[:, 3 * w:3 * w + 2 * DECAY_RANK]
    la = z[:, 3 * w + 2 * DECAY_RANK:3 * w + 2 * DECAY_RANK + 2 * ICLR_RANK]
    lg = z[:, 3 * w + 2 * DECAY_RANK + 2 * ICLR_RANK:]

    x = w0_ref[...] + _mm_hilo(jnp.tanh(lw), w2h_ref[...], w2l_ref[...])
    log_decay = -math.exp(-0.5) / (1.0 + jnp.exp(-x))
    iclr = _sigmoid(a0_ref[...] + _mm_hilo(la, a2h_ref[...], a2l_ref[...]))
    gate = _mm_hilo(_sigmoid(lg), g2h_ref[...], g2l_ref[...])

    seg = seg_ref[...]
    kk = k * kk_w_ref[...]
    kk = kk * lax.rsqrt(jnp.maximum(_seg_sum(kk * kk, seg), 1e-24))
    bonus = _seg_sum(r * k * rk_w_ref[...], seg) * v

    r_ref[0] = r
    v_ref[0] = v
    kk_ref[0] = kk
    lw_ref[0] = log_decay
    gate_ref[0] = gate
    bonus_ref[0] = bonus
    ka = ka_w_ref[...]
    for d in range(2):
        a_d = iclr[:, d * w:(d + 1) * w]
        kd_ref[0, :, d * w:(d + 1) * w] = k * (1.0 + (a_d - 1.0) * ka)
        be_ref[0, :, d * w:(d + 1) * w] = kk * a_d


def _block_diag2(m):
    z = jnp.zeros_like(m[0])
    return jnp.concatenate([jnp.concatenate([m[0], z], axis=1), jnp.concatenate([z, m[1]], axis=1)], axis=0)


def _rw_prep(xb, w_zr, mu_prev, mu_next, w0, w2, a0, a2, g2, k_k, k_a, r_k):
    b, s, dm = xb.shape
    c = w_zr.shape[1]
    ts = RW_PREP_TILE
    w = RW_WIDTH
    seg = _head_segment_matrix()
    halo = RW_HALO
    nhalo = s // halo

    def vec(width):
        return pl.BlockSpec((1, width), lambda bi, i: (0, 0))

    def mat(rows, cols):
        return pl.BlockSpec((rows, cols), lambda bi, i: (0, 0))

    def out(width):
        return pl.BlockSpec((1, ts, width), lambda bi, i: (bi, i, 0))

    shapes = [jax.ShapeDtypeStruct((b, s, width), F32) for width in (w, w, w, 2 * w, 2 * w, 2 * w, w, w)]
    return pl.pallas_call(
        _rw_prep_kernel,
        grid=(b, s // ts),
        in_specs=[pl.BlockSpec((1, ts, dm), lambda bi, i: (bi, i, 0)),
                  pl.BlockSpec((1, halo, dm), lambda bi, i: (bi, jnp.maximum(i * (ts // halo) - 1, 0), 0)),
                  pl.BlockSpec((1, halo, dm), lambda bi, i: (bi, jnp.minimum((i + 1) * (ts // halo), nhalo - 1), 0)),
                  mat(dm, c), vec(c), vec(c), vec(2 * w), mat(4 * DECAY_RANK, 2 * w), mat(2 * DECAY_RANK, 2 * w),
                  vec(2 * w), mat(4 * ICLR_RANK, 2 * w), mat(2 * ICLR_RANK, 2 * w),
                  mat(2 * GATE_RANK, w), mat(GATE_RANK, w), vec(w), vec(w), vec(w), mat(SEG_LANES, SEG_LANES)],
        out_specs=[out(w), out(w), out(w), out(2 * w), out(2 * w), out(2 * w), out(w), out(w)],
        out_shape=shapes,
        compiler_params=_cparams("parallel", "arbitrary"),
        name="rw_prep",
    )(xb, xb, xb, w_zr, mu_prev.reshape(1, -1), mu_next.reshape(1, -1), w0.reshape(1, -1),
      *_hilo_weights(_block_diag2(w2)),
      a0.reshape(1, -1), *_hilo_weights(_block_diag2(a2)), *_hilo_weights(g2),
      k_k.reshape(1, -1), k_a.reshape(1, -1), r_k.reshape(1, -1), seg)


def _unit_triangular_inverses(mats):
    n = mats[0].shape[0]
    ti = lax.broadcasted_iota(jnp.int32, (n, n), 0)
    si = lax.broadcasted_iota(jnp.int32, (n, n), 1)

    def same_block(m):
        return (ti // m) == (si // m)

    mm = functools.partial(_mm, dims=NN, passes=INV_PASSES)
    eye = (ti == si).astype(F32)
    base = same_block(RW_INV_BASE)
    diag = [jnp.where(base, a, 0.0) for a in mats]
    power = [mm(a, a) for a in diag]
    inv = [mm(eye + a, eye + p) for a, p in zip(diag, power)]
    m = 4
    while m < RW_INV_BASE:
        power = [mm(p, p) for p in power]
        inv = [mm(t, eye + p) for t, p in zip(inv, power)]
        m *= 2
    m = RW_INV_BASE
    while m < n:
        ring = jnp.logical_and(same_block(2 * m), jnp.logical_not(same_block(m)))
        tmp = [mm(jnp.where(ring, a, 0.0), t) for a, t in zip(mats, inv)]
        inv = [t + mm(t, x) for t, x in zip(inv, tmp)]
        m *= 2
    return inv


def _rw_scan_kernel(rf_ref, vf_ref, kkf_ref, lwf_ref, kdf_ref, bef_ref,
                    rb_ref, vb_ref, kkb_ref, lwb_ref, kdb_ref, beb_ref, yf_ref, yb_ref, state_ref):
    @pl.when(pl.program_id(1) == 0)
    def _():
        state_ref[...] = jnp.zeros_like(state_ref)

    n = RW_CHUNK
    hd = RW_HEAD_DIM
    ti = lax.broadcasted_iota(jnp.int32, (2 * n, 2 * n), 0)
    si = lax.broadcasted_iota(jnp.int32, (2 * n, 2 * n), 1)

    items = []
    for d, (r_ref, v_ref, kk_ref, lw_ref, kd_ref, be_ref) in enumerate((
            (rf_ref, vf_ref, kkf_ref, lwf_ref, kdf_ref, bef_ref),
            (rb_ref, vb_ref, kkb_ref, lwb_ref, kdb_ref, beb_ref))):
        lag = (ti % n - si % n) * (1 - 2 * d)
        keep = lag >= jnp.where(ti < n, 1, 0)
        tri = (lag[:n, :n] >= 0).astype(BF16)
        for order in range(RW_CHUNKS_PER_STEP):
            pos = order if d == 0 else RW_CHUNKS_PER_STEP - 1 - order
            rows = slice(pos * n, (pos + 1) * n)
            for bb in range(RW_BATCH_BLOCK):
                lw = lw_ref[bb, rows, :]
                cum = sum(jnp.dot(tri, part, preferred_element_type=F32) for part in _split_bf16(lw, 3))
                e_neg = jnp.exp(-cum)
                ar_all = jnp.concatenate([-kk_ref[bb, rows, :] * jnp.exp(cum - lw),
                                          r_ref[bb, rows, :] * jnp.exp(cum)], axis=0)
                bk_all = jnp.concatenate([be_ref[bb, rows, :] * e_neg, kd_ref[bb, rows, :] * e_neg], axis=0)
                v_all = v_ref[bb, rows, :]
                e_tot = jnp.exp(jnp.sum(lw, axis=0, keepdims=True))
                for h in range(RW_HEADS):
                    cols = slice(h * hd, (h + 1) * hd)
                    items.append(dict(order=order, chain=(bb, d, h), rows=rows, cols=cols, keep=keep,
                                      ar=ar_all[:, cols], bk=bk_all[:, cols], v=v_all[:, cols], decay=e_tot[:, cols]))

    for it in items:
        it["m"] = jnp.where(it["keep"], _mm(it["ar"], it["bk"], NT, MIX_PASSES), 0.0)
    for it, inv in zip(items, _unit_triangular_inverses([it["m"][:n, :n] for it in items])):
        it["inv"] = inv

    state = {}
    for bb in range(RW_BATCH_BLOCK):
        for d in range(2):
            for h in range(RW_HEADS):
                state[(bb, d, h)] = state_ref[bb, d, h]
    for order in range(RW_CHUNKS_PER_STEP):
        group = [it for it in items if it["order"] == order]
        for it in group:
            it["xy"] = (_mm(it["ar"], state[it["chain"]], NT, MIX_PASSES)
                        + _mm(it["m"][:, n:], it["v"], NN, MIX_PASSES))
        for it in group:
            it["u"] = _mm(it["inv"], it["xy"][:n], NN, INV_PASSES)
        for it in group:
            bb, d, _ = it["chain"]
            y = it["xy"][n:] + _mm(it["m"][n:, :n], it["u"], NN, MIX_PASSES)
            (yf_ref, yb_ref)[d][bb, it["rows"], it["cols"]] = y
        for it in group:
            upd = _mm(jnp.concatenate([it["u"], it["v"]], axis=0), it["bk"], TN, MIX_PASSES)
            state[it["chain"]] = (state[it["chain"]] + upd) * it["decay"]
    for (bb, d, h), st in state.items():
        state_ref[bb, d, h] = st


def _rw_scan(r, v, kk, lw, kd, be):
    b, s, w = r.shape
    rows = RW_CHUNK * RW_CHUNKS_PER_STEP
    nblk = s // rows
    nb = RW_BATCH_BLOCK

    def spec(d, col):
        if d == 0:
            return pl.BlockSpec((nb, rows, w), lambda bi, c: (bi, c, col))
        return pl.BlockSpec((nb, rows, w), lambda bi, c: (bi, nblk - 1 - c, col))

    in_specs = []
    for d in range(2):
        in_specs += [spec(d, 0), spec(d, 0), spec(d, 0), spec(d, d), spec(d, d), spec(d, d)]
    return pl.pallas_call(
        _rw_scan_kernel,
        grid=(b // nb, nblk),
        in_specs=in_specs,
        out_specs=[spec(0, 0), spec(1, 0)],
        out_shape=[jax.ShapeDtypeStruct((b, s, w), F32)] * 2,
        scratch_shapes=[pltpu.VMEM((nb, 2, RW_HEADS, RW_HEAD_DIM, RW_HEAD_DIM), F32)],
        compiler_params=_cparams("parallel", "arbitrary"),
        name="rw_scan",
    )(r, v, kk, lw, kd, be, r, v, kk, lw, kd, be)


def _rw_finish(yf, yb, gate, bonus, gn_g, gn_b, seg):
    y = yf + yb
    mu = _seg_sum(y, seg) * (1.0 / RW_HEAD_DIM)
    yc = y - mu
    var = _seg_sum(yc * yc, seg) * (1.0 / RW_HEAD_DIM)
    yn = yc * lax.rsqrt(var + GN_EPS) * gn_g + gn_b
    return ((yn + bonus) * gate).astype(BF16)


def _merge_kernel(ya_ref, yb_ref, yf_ref, yr_ref, rgate_ref, bonus_ref, gn_g_ref, gn_b_ref, seg_ref,
                  wg_ref, wb_ref, bg_ref, wo_ref, x_ref, g_ref, b_ref, o_ref):
    x = x_ref[...]
    xb = x.astype(BF16)
    yc = _rw_finish(yf_ref[...], yr_ref[...], rgate_ref[...], bonus_ref[...], gn_g_ref[...], gn_b_ref[...],
                    seg_ref[...])
    merged = None
    for n, y in enumerate((ya_ref[...], yb_ref[...], yc)):
        cols = slice(n * D_MODEL, (n + 1) * D_MODEL)
        gate_logits = jnp.dot(xb, wg_ref[:, cols], preferred_element_type=F32) + bg_ref[:, cols]
        branch = jnp.dot(y, wb_ref[n], preferred_element_type=F32)
        term = _sigmoid(gate_logits) * branch
        merged = term if merged is None else merged + term
    mix = jnp.dot(merged.astype(BF16), wo_ref[...], preferred_element_type=F32)
    o_ref[...] = _layer_norm(ALPHA * x + mix, g_ref[...], b_ref[...])


def _merge(ya, yb, yf, yr, rgate, bonus, gn_g, gn_b, w_gate, w_branch, b_gate, w_out, x, g, b):
    n = x.shape[0]
    tm = ROW_TILE
    ytile = pl.BlockSpec((tm, BRANCH_WIDTH), lambda i: (i, 0))
    row = pl.BlockSpec((tm, D_MODEL), lambda i: (i, 0))
    vec = pl.BlockSpec((1, D_MODEL), lambda i: (0, 0))
    hvec = pl.BlockSpec((1, RW_WIDTH), lambda i: (0, 0))
    return pl.pallas_call(
        _merge_kernel,
        grid=(n // tm,),
        in_specs=[ytile, ytile, ytile, ytile, ytile, ytile, hvec, hvec,
                  pl.BlockSpec((SEG_LANES, SEG_LANES), lambda i: (0, 0)),
                  pl.BlockSpec((D_MODEL, 3 * D_MODEL), lambda i: (0, 0)),
                  pl.BlockSpec((3, BRANCH_WIDTH, D_MODEL), lambda i: (0, 0, 0)),
                  pl.BlockSpec((1, 3 * D_MODEL), lambda i: (0, 0)),
                  pl.BlockSpec((D_MODEL, D_MODEL), lambda i: (0, 0)),
                  row, vec, vec],
        out_specs=row,
        out_shape=jax.ShapeDtypeStruct((n, D_MODEL), F32),
        compiler_params=_cparams("parallel"),
        name="merge",
    )(ya, yb, yf, yr, rgate, bonus, gn_g.reshape(1, -1), gn_b.reshape(1, -1), _head_segment_matrix(),
      w_gate, w_branch.astype(BF16), b_gate.reshape(1, -1), w_out.astype(BF16), x, g.reshape(1, -1), b.reshape(1, -1))


def _mlp_kernel(x_ref, wu_ref, wd_ref, g_ref, b_ref, o_ref, ob_ref):
    x = x_ref[...]
    xb = x.astype(BF16)
    acc = None
    for j in range(D_FF // MLP_FF_SLICE):
        ff = slice(j * MLP_FF_SLICE, (j + 1) * MLP_FF_SLICE)
        hid = jnp.maximum(jnp.dot(xb, wu_ref[:, ff], preferred_element_type=F32), 0.0)
        part = jnp.dot((hid * hid).astype(BF16), wd_ref[ff, :], preferred_element_type=F32)
        acc = part if acc is None else acc + part
    y = _layer_norm(ALPHA * x + acc, g_ref[...], b_ref[...])
    o_ref[...] = y
    ob_ref[...] = y.astype(BF16)


def _mlp(x, w_up, w_down, g, b):
    n = x.shape[0]
    tm = MLP_ROW_TILE
    row = pl.BlockSpec((tm, D_MODEL), lambda i: (i, 0))
    vec = pl.BlockSpec((1, D_MODEL), lambda i: (0, 0))
    resident = dict(pipeline_mode=pl.Buffered(1))
    return pl.pallas_call(
        _mlp_kernel,
        grid=(n // tm,),
        in_specs=[row,
                  pl.BlockSpec((D_MODEL, D_FF), lambda i: (0, 0), **resident),
                  pl.BlockSpec((D_FF, D_MODEL), lambda i: (0, 0), **resident),
                  vec, vec],
        out_specs=[row, row],
        out_shape=[jax.ShapeDtypeStruct((n, D_MODEL), F32), jax.ShapeDtypeStruct((n, D_MODEL), BF16)],
        compiler_params=_cparams("parallel"),
        name="mlp",
    )(x, w_up.astype(BF16), w_down.astype(BF16), g.reshape(1, -1), b.reshape(1, -1))


def kernel(x, ln0_g, ln0_b, w_in, pool_w, pool_scale, da_lambda, da_subln_g, rel_bias, rw_mu_prev, rw_mu_next, rw_w0, rw_w2, rw_a0, rw_a2, rw_g2, rw_k_k, rw_k_a, rw_r_k, rw_gn_g, rw_gn_b, w_branch, b_gate, w_out, ln1_g, ln1_b, w_up, w_down, ln2_g, ln2_b):
    b, s, dm = x.shape
    n = b * s
    assert dm == D_MODEL and n % MLP_ROW_TILE == 0 and n % ROW_TILE == 0
    assert s % ATTN_Q_TILE == 0 and s % ROW_TILE == 0 and s % RW_PREP_TILE == 0
    assert b % RW_BATCH_BLOCK == 0 and s % (RW_CHUNK * RW_CHUNKS_PER_STEP) == 0
    c_uqkv = 4 * BRANCH_WIDTH
    c_zr = c_uqkv + RW_SHIFT_WIDTH

    strip = _bias_strip(rel_bias, s)
    xf, xb = _ln0(x.reshape(n, dm), ln0_g, ln0_b)
    for l in range(DEPTH):
        wl = w_in[l].astype(BF16)
        u, q, kv = (t.reshape(b, s, -1) for t in _proj(xb, wl[:, :c_uqkv]))

        y_a = _pool(u, pool_w[l], pool_scale[l])
        lam_init = 0.8 - 0.6 * math.exp(-0.3 * l)
        y_b = _diff_attention(q, kv, strip, da_lambda[l], da_subln_g[l], lam_init)
        r, v, kk, lw, kd, be, gate, bonus = _rw_prep(
            xb.reshape(b, s, dm), wl[:, c_uqkv:c_zr], rw_mu_prev[l], rw_mu_next[l], rw_w0[l], rw_w2[l], rw_a0[l], rw_a2[l], rw_g2[l],
            rw_k_k[l], rw_k_a[l], rw_r_k[l])
        yf, yr = _rw_scan(r, v, kk, lw, kd, be)

        flat = lambda t: t.reshape(n, -1)
        xf = _merge(flat(y_a), flat(y_b), flat(yf), flat(yr), flat(gate), flat(bonus), rw_gn_g[l], rw_gn_b[l],
                    wl[:, c_zr:], w_branch[l], b_gate[l], w_out[l], xf, ln1_g[l], ln1_b[l])
        xf, xb = _mlp(xf, w_up[l], w_down[l], ln2_g[l], ln2_b[l])
    return xf.reshape(b, s, dm)
```

```python
import functools
import math

import numpy as np
import jax
import jax.numpy as jnp
from jax import lax
from jax.experimental import pallas as pl
from jax.experimental.pallas import tpu as pltpu

D_MODEL = 1024
DEPTH = 4
BRANCH_WIDTH = 512
POOL_WINDOWS = (2, 4, 8, 16)
POOL_GROUP_DIM = 128
DA_HEAD_DIM = 64
DA_HEADS = 4
REL_BUCKETS = 32
REL_MAX_DIST = 128
RW_WIDTH = 512
RW_HEAD_DIM = 64
RW_HEADS = 8
DECAY_RANK = 64
ICLR_RANK = 64
GATE_RANK = 128
RW_SHIFT_WIDTH = 3 * RW_WIDTH + 2 * DECAY_RANK + 2 * ICLR_RANK + GATE_RANK
GN_EPS = 64e-5
D_FF = 4 * D_MODEL
LN_EPS = 1e-5
ALPHA = (2.0 * DEPTH) ** 0.25
LOG2E = math.log2(math.e)

VMEM_LIMIT_BYTES = 56 * 1024 * 1024

ROW_TILE = 512
MLP_ROW_TILE = 512
MLP_FF_SLICE = 1024
ATTN_Q_TILE = 256
ATTN_HEADS_PER_STEP = 4
RW_PREP_TILE = 512
RW_HALO = 16
RW_CHUNK = 64
RW_CHUNKS_PER_STEP = 2
RW_BATCH_BLOCK = 2
RW_INV_BASE = 8

F32 = jnp.float32
BF16 = jnp.bfloat16
NN = (((1,), (0,)), ((), ()))
NT = (((1,), (1,)), ((), ()))
TN = (((0,), (0,)), ((), ()))


def _cparams(*sem):
    return pltpu.CompilerParams(dimension_semantics=sem, vmem_limit_bytes=VMEM_LIMIT_BYTES)


def _layer_norm(x, g, b):
    mu = jnp.mean(x, axis=-1, keepdims=True)
    xc = x - mu
    var = jnp.mean(xc * xc, axis=-1, keepdims=True)
    return xc * lax.rsqrt(var + LN_EPS) * g + b


def _split_bf16(x, terms):
    parts = []
    for _ in range(terms):
        p = x.astype(BF16)
        parts.append(p)
        x = x - p.astype(F32)
    return parts


def _hilo_weights(b):
    hi, lo = _split_bf16(b, 2)
    return jnp.concatenate([hi, hi], axis=0), lo


def _mm_hilo(a, b_hihi, b_lo):
    ah, al = _split_bf16(a, 2)
    return (jnp.dot(jnp.concatenate([ah, al], axis=1), b_hihi, preferred_element_type=F32)
            + jnp.dot(ah, b_lo, preferred_element_type=F32))


def _mm(a, b, dims):
    return lax.dot_general(a.astype(BF16), b.astype(BF16), dims, preferred_element_type=F32)


def _ln0_kernel(x_ref, g_ref, b_ref, o_ref, ob_ref):
    y = _layer_norm(x_ref[...], g_ref[...], b_ref[...])
    o_ref[...] = y
    ob_ref[...] = y.astype(BF16)


def _ln0(x2, g, b):
    n = x2.shape[0]
    row = pl.BlockSpec((ROW_TILE, D_MODEL), lambda i: (i, 0))
    vec = pl.BlockSpec((1, D_MODEL), lambda i: (0, 0))
    return pl.pallas_call(
        _ln0_kernel,
        grid=(n // ROW_TILE,),
        in_specs=[row, vec, vec],
        out_specs=[row, row],
        out_shape=[jax.ShapeDtypeStruct((n, D_MODEL), F32), jax.ShapeDtypeStruct((n, D_MODEL), BF16)],
        compiler_params=_cparams("parallel"),
        name="ln0",
    )(x2, g.reshape(1, -1), b.reshape(1, -1))


def _proj_kernel(x_ref, w_ref, u_ref, q_ref, kv_ref):
    x = x_ref[...]
    w = BRANCH_WIDTH
    u_ref[...] = jnp.dot(x, w_ref[:, :w], preferred_element_type=F32)
    q = jnp.dot(x, w_ref[:, w:2 * w], preferred_element_type=F32)
    q_ref[...] = (q * (DA_HEAD_DIM ** -0.5 * LOG2E)).astype(BF16)
    kv_ref[...] = jnp.dot(x, w_ref[:, 2 * w:], preferred_element_type=F32).astype(BF16)


def _proj(xb, w_uqkv):
    n, k = xb.shape
    w = BRANCH_WIDTH

    def out(width):
        return pl.BlockSpec((ROW_TILE, width), lambda i: (i, 0))

    return pl.pallas_call(
        _proj_kernel,
        grid=(n // ROW_TILE,),
        in_specs=[pl.BlockSpec((ROW_TILE, k), lambda i: (i, 0)),
                  pl.BlockSpec((k, 4 * w), lambda i: (0, 0))],
        out_specs=[out(w), out(w), out(2 * w)],
        out_shape=[jax.ShapeDtypeStruct((n, w), F32), jax.ShapeDtypeStruct((n, w), BF16),
                   jax.ShapeDtypeStruct((n, 2 * w), BF16)],
        compiler_params=_cparams("parallel"),
        name="proj_uqkv",
    )(xb, w_uqkv)


def _pool_kernel(u_ref, w_ref, s_ref, o_ref):
    s = u_ref.shape[1]
    t = lax.broadcasted_iota(jnp.int32, (s, POOL_GROUP_DIM), 0)
    for gi, win in enumerate(POOL_WINDOWS):
        left = win // 2
        right = win - 1 - left
        cols = slice(gi * POOL_GROUP_DIM, (gi + 1) * POOL_GROUP_DIM)
        x = u_ref[0, :, cols]
        acc = x
        for d in range(-left, right + 1):
            if d == 0:
                continue
            shifted = pltpu.roll(x, (-d) % s, axis=0)
            valid = jnp.logical_and(t + d >= 0, t + d < s)
            acc = acc + jnp.where(valid, shifted, 0.0)
        cnt = (jnp.minimum(t + right + 1, s) - jnp.maximum(t - left, 0)).astype(F32)
        mixed = acc / cnt - x
        y = jnp.dot(mixed.astype(BF16), w_ref[gi], preferred_element_type=F32)
        o_ref[0, :, cols] = (y * s_ref[:, cols]).astype(BF16)


def _pool(u, pool_w, pool_scale):
    b, s, _ = u.shape
    return pl.pallas_call(
        _pool_kernel,
        grid=(b,),
        in_specs=[pl.BlockSpec((1, s, BRANCH_WIDTH), lambda i: (i, 0, 0)),
                  pl.BlockSpec((len(POOL_WINDOWS), POOL_GROUP_DIM, POOL_GROUP_DIM), lambda i: (0, 0, 0)),
                  pl.BlockSpec((1, BRANCH_WIDTH), lambda i: (0, 0))],
        out_specs=pl.BlockSpec((1, s, BRANCH_WIDTH), lambda i: (i, 0, 0)),
        out_shape=jax.ShapeDtypeStruct((b, s, BRANCH_WIDTH), BF16),
        compiler_params=_cparams("parallel"),
        name="pool",
    )(u, pool_w.astype(BF16), pool_scale.reshape(1, -1))


def _rel_bucket_np(rel):
    half = REL_BUCKETS // 2
    max_exact = half // 2
    n = np.abs(rel)
    nf = np.maximum(n, 1).astype(np.float64)
    large = max_exact + (np.log(nf / max_exact) / math.log(REL_MAX_DIST / max_exact)
                         * (half - max_exact)).astype(np.int32)
    large = np.minimum(large, half - 1)
    return (np.where(rel > 0, half, 0) + np.where(n < max_exact, n, large)).astype(np.int32)


def _bias_strip_kernel(tab_ref, bk_ref, o_ref):
    h = pl.program_id(0)
    bk = bk_ref[...]
    acc = jnp.zeros(bk.shape, F32)
    for j in range(REL_BUCKETS):
        acc = jnp.where(bk == j, tab_ref[j * DA_HEADS + h] * LOG2E, acc)
    o_ref[0] = acc


def _bias_strip(rel_bias, s):
    tq = ATTN_Q_TILE
    w = 2 * s - tq
    rel = np.arange(w)[None, :] - np.arange(tq)[:, None] - (s - tq)
    bucket = jnp.asarray(_rel_bucket_np(rel))
    return pl.pallas_call(
        _bias_strip_kernel,
        grid=(DA_HEADS,),
        in_specs=[pl.BlockSpec(memory_space=pltpu.SMEM),
                  pl.BlockSpec((tq, w), lambda h: (0, 0))],
        out_specs=pl.BlockSpec((1, tq, w), lambda h: (h, 0, 0)),
        out_shape=jax.ShapeDtypeStruct((DA_HEADS, tq, w), F32),
        compiler_params=_cparams("arbitrary"),
        name="bias_strip",
    )(rel_bias.reshape(-1), bucket)


def _attn_kernel(lam0_ref, lamv_ref, q_ref, k_ref, v_ref, strip_ref, g_ref, o_ref):
    tq = q_ref.shape[1]
    s = k_ref.shape[1]
    hw = 2 * DA_HEAD_DIM
    qi = pl.program_id(2)
    nq = pl.num_programs(2)
    lam_init = lam0_ref[0]
    lv = lamv_ref[...]
    lam = (jnp.exp(jnp.sum(lv[0:1] * lv[1:2], axis=-1, keepdims=True))
           - jnp.exp(jnp.sum(lv[2:3] * lv[3:4], axis=-1, keepdims=True)) + lam_init)
    off = pl.multiple_of((nq - 1 - qi) * tq, tq)
    lane = lax.broadcasted_iota(jnp.int32, (tq, hw), 1)

    def logits(h, first_half):
        cols = slice(h * hw, (h + 1) * hw)
        q = q_ref[0, :, cols]
        qm = jnp.where((lane < DA_HEAD_DIM) == first_half, q, jnp.zeros_like(q))
        return (lax.dot_general(qm, k_ref[0, :, cols], NT, preferred_element_type=F32)
                + strip_ref[h, :, pl.ds(off, s)])

    def softmax_times_v(h, lg):
        m = jnp.max(lg, axis=-1, keepdims=True)
        e = jnp.exp2(lg - m)
        l = jnp.sum(e, axis=-1, keepdims=True)
        return jnp.dot(e.astype(BF16), v_ref[0, :, h * hw:(h + 1) * hw], preferred_element_type=F32) / l

    streams = [(h, first) for h in range(ATTN_HEADS_PER_STEP) for first in (True, False)]
    pending = logits(*streams[0])
    outs = []
    for idx, (h, _) in enumerate(streams):
        lg = pending
        if idx + 1 < len(streams):
            pending = logits(*streams[idx + 1])
        outs.append(softmax_times_v(h, lg))
    for h in range(ATTN_HEADS_PER_STEP):
        o = outs[2 * h] - lam * outs[2 * h + 1]
        o = o * lax.rsqrt(jnp.mean(o * o, axis=-1, keepdims=True) + 1e-5) * g_ref[...]
        o_ref[0, :, h * hw:(h + 1) * hw] = (o * (1.0 - lam_init)).astype(BF16)


def _diff_attention(q, kv, strip, da_lambda, subln_g, lam_init):
    b, s, _ = q.shape
    tq = ATTN_Q_TILE
    nh = ATTN_HEADS_PER_STEP
    hw = 2 * DA_HEAD_DIM
    bw = nh * hw
    v_blk0 = BRANCH_WIDTH // bw
    return pl.pallas_call(
        _attn_kernel,
        grid=(b, DA_HEADS // nh, s // tq),
        in_specs=[pl.BlockSpec(memory_space=pltpu.SMEM),
                  pl.BlockSpec((4, DA_HEAD_DIM), lambda bi, hp, qi: (0, 0)),
                  pl.BlockSpec((1, tq, bw), lambda bi, hp, qi: (bi, qi, hp)),
                  pl.BlockSpec((1, s, bw), lambda bi, hp, qi: (bi, 0, hp)),
                  pl.BlockSpec((1, s, bw), lambda bi, hp, qi: (bi, 0, v_blk0 + hp)),
                  pl.BlockSpec((nh, tq, strip.shape[2]), lambda bi, hp, qi: (hp, 0, 0),
                               pipeline_mode=pl.Buffered(1 if nh == DA_HEADS else 2)),
                  pl.BlockSpec((1, hw), lambda bi, hp, qi: (0, 0))],
        out_specs=pl.BlockSpec((1, tq, bw), lambda bi, hp, qi: (bi, qi, hp)),
        out_shape=jax.ShapeDtypeStruct((b, s, BRANCH_WIDTH), BF16),
        compiler_params=_cparams("parallel", "parallel", "arbitrary"),
        name="diff_attn",
    )(jnp.full((1,), lam_init, F32), da_lambda, q, kv, kv, strip, subln_g.reshape(1, -1))


SEG_LANES = 256


def _head_segment_matrix():
    heads = SEG_LANES // RW_HEAD_DIM
    return jnp.asarray(np.kron(np.eye(heads), np.ones((RW_HEAD_DIM, RW_HEAD_DIM))), dtype=BF16)


def _seg_sum(x, seg):
    parts = _split_bf16(x, 2)
    blocks = []
    for c in range(x.shape[1] // SEG_LANES):
        cols = slice(c * SEG_LANES, (c + 1) * SEG_LANES)
        blocks.append(sum(jnp.dot(p[:, cols], seg, preferred_element_type=F32) for p in parts))
    return jnp.concatenate(blocks, axis=1)


def _sigmoid(x):
    return 1.0 / (1.0 + jnp.exp(-x))


def _rw_prep_kernel(x_ref, xp_ref, xn_ref, wz_ref, mup_ref, mun_ref, w0_ref, w2h_ref, w2l_ref, a0_ref, a2h_ref,
                    a2l_ref, g2h_ref, g2l_ref, kk_w_ref, ka_w_ref, rk_w_ref, seg_ref,
                    r_ref, v_ref, kk_ref, lw_ref, kd_ref, be_ref, gate_ref, bonus_ref):
    i = pl.program_id(1)
    n = pl.num_programs(1)
    ts = x_ref.shape[1]
    halo = RW_HALO
    xe = jnp.concatenate([xp_ref[0], x_ref[0], xn_ref[0]], axis=0)
    ze = jnp.dot(xe, wz_ref[...], preferred_element_type=F32)
    z = ze[halo:halo + ts]
    slab = 8
    t = lax.broadcasted_iota(jnp.int32, (slab, 1), 0)
    prev_row = jnp.where(i > 0, ze[halo - 1:halo], 0.0)
    next_row = jnp.where(i < n - 1, ze[halo + ts:halo + ts + 1], 0.0)
    prev = pltpu.roll(z, 1, axis=0)
    prev = jnp.concatenate([jnp.where(t == 0, prev_row, prev[:slab]), prev[slab:]], axis=0)
    nxt = pltpu.roll(z, ts - 1, axis=0)
    nxt = jnp.concatenate([nxt[:ts - slab], jnp.where(t == slab - 1, next_row, nxt[ts - slab:])], axis=0)
    z = z + mup_ref[...] * (prev - z) + mun_ref[...] * (nxt - z)

    w = RW_WIDTH
    r = z[:, 0:w]
    k = z[:, w:2 * w]
    v = z[:, 2 * w:3 * w]
    lw = z[:, 3 * w:3 * w + 2 * DECAY_RANK]
    la = z[:, 3 * w + 2 * DECAY_RANK:3 * w + 2 * DECAY_RANK + 2 * ICLR_RANK]
    lg = z[:, 3 * w + 2 * DECAY_RANK + 2 * ICLR_RANK:]

    x = w0_ref[...] + _mm_hilo(jnp.tanh(lw), w2h_ref[...], w2l_ref[...])
    log_decay = -math.exp(-0.5) / (1.0 + jnp.exp(-x))
    iclr = _sigmoid(a0_ref[...] + _mm_hilo(la, a2h_ref[...], a2l_ref[...]))
    gate = _mm_hilo(_sigmoid(lg), g2h_ref[...], g2l_ref[...])

    seg = seg_ref[...]
    kk = k * kk_w_ref[...]
    kk = kk * lax.rsqrt(jnp.maximum(_seg_sum(kk * kk, seg), 1e-24))
    bonus = _seg_sum(r * k * rk_w_ref[...], seg) * v

    r_ref[0] = r
    v_ref[0] = v
    kk_ref[0] = kk
    lw_ref[0] = log_decay
    gate_ref[0] = gate
    bonus_ref[0] = bonus
    ka = ka_w_ref[...]
    for d in range(2):
        a_d = iclr[:, d * w:(d + 1) * w]
        kd_ref[0, :, d * w:(d + 1) * w] = k * (1.0 + (a_d - 1.0) * ka)
        be_ref[0, :, d * w:(d + 1) * w] = kk * a_d


def _block_diag2(m):
    z = jnp.zeros_like(m[0])
    return jnp.concatenate([jnp.concatenate([m[0], z], axis=1), jnp.concatenate([z, m[1]], axis=1)], axis=0)


def _rw_prep(xb, w_zr, mu_prev, mu_next, w0, w2, a0, a2, g2, k_k, k_a, r_k):
    b, s, dm = xb.shape
    c = w_zr.shape[1]
    ts = RW_PREP_TILE
    w = RW_WIDTH
    seg = _head_segment_matrix()
    halo = RW_HALO
    nhalo = s // halo

    def vec(width):
        return pl.BlockSpec((1, width), lambda bi, i: (0, 0))

    def mat(rows, cols):
        return pl.BlockSpec((rows, cols), lambda bi, i: (0, 0))

    def out(width):
        return pl.BlockSpec((1, ts, width), lambda bi, i: (bi, i, 0))

    shapes = [jax.ShapeDtypeStruct((b, s, width), F32) for width in (w, w, w, 2 * w, 2 * w, 2 * w, w, w)]
    return pl.pallas_call(
        _rw_prep_kernel,
        grid=(b, s // ts),
        in_specs=[pl.BlockSpec((1, ts, dm), lambda bi, i: (bi, i, 0)),
                  pl.BlockSpec((1, halo, dm), lambda bi, i: (bi, jnp.maximum(i * (ts // halo) - 1, 0), 0)),
                  pl.BlockSpec((1, halo, dm), lambda bi, i: (bi, jnp.minimum((i + 1) * (ts // halo), nhalo - 1), 0)),
                  mat(dm, c), vec(c), vec(c), vec(2 * w), mat(4 * DECAY_RANK, 2 * w), mat(2 * DECAY_RANK, 2 * w),
                  vec(2 * w), mat(4 * ICLR_RANK, 2 * w), mat(2 * ICLR_RANK, 2 * w),
                  mat(2 * GATE_RANK, w), mat(GATE_RANK, w), vec(w), vec(w), vec(w), mat(SEG_LANES, SEG_LANES)],
        out_specs=[out(w), out(w), out(w), out(2 * w), out(2 * w), out(2 * w), out(w), out(w)],
        out_shape=shapes,
        compiler_params=_cparams("parallel", "arbitrary"),
        name="rw_prep",
    )(xb, xb, xb, w_zr, mu_prev.reshape(1, -1), mu_next.reshape(1, -1), w0.reshape(1, -1),
      *_hilo_weights(_block_diag2(w2)),
      a0.reshape(1, -1), *_hilo_weights(_block_diag2(a2)), *_hilo_weights(g2),
      k_k.reshape(1, -1), k_a.reshape(1, -1), r_k.reshape(1, -1), seg)


def _unit_triangular_inverses(mats):
    n = mats[0].shape[0]
    ti = lax.broadcasted_iota(jnp.int32, (n, n), 0)
    si = lax.broadcasted_iota(jnp.int32, (n, n), 1)

    def same_block(m):
        return (ti // m) == (si // m)

    mm = functools.partial(_mm, dims=NN)
    eye = (ti == si).astype(F32)
    base = same_block(RW_INV_BASE)
    diag = [jnp.where(base, a, 0.0) for a in mats]
    power = [mm(a, a) for a in diag]
    inv = [mm(eye + a, eye + p) for a, p in zip(diag, power)]
    m = 4
    while m < RW_INV_BASE:
        power = [mm(p, p) for p in power]
        inv = [mm(t, eye + p) for t, p in zip(inv, power)]
        m *= 2
    m = RW_INV_BASE
    while m < n:
        ring = jnp.logical_and(same_block(2 * m), jnp.logical_not(same_block(m)))
        tmp = [mm(jnp.where(ring, a, 0.0), t) for a, t in zip(mats, inv)]
        inv = [t + mm(t, x) for t, x in zip(inv, tmp)]
        m *= 2
    return inv


def _rw_scan_kernel(rf_ref, vf_ref, kkf_ref, lwf_ref, kdf_ref, bef_ref,
                    rb_ref, vb_ref, kkb_ref, lwb_ref, kdb_ref, beb_ref, yf_ref, yb_ref, state_ref):
    @pl.when(pl.program_id(1) == 0)
    def _():
        state_ref[...] = jnp.zeros_like(state_ref)

    n = RW_CHUNK
    hd = RW_HEAD_DIM
    ti = lax.broadcasted_iota(jnp.int32, (2 * n, 2 * n), 0)
    si = lax.broadcasted_iota(jnp.int32, (2 * n, 2 * n), 1)

    items = []
    for d, (r_ref, v_ref, kk_ref, lw_ref, kd_ref, be_ref) in enumerate((
            (rf_ref, vf_ref, kkf_ref, lwf_ref, kdf_ref, bef_ref),
            (rb_ref, vb_ref, kkb_ref, lwb_ref, kdb_ref, beb_ref))):
        lag = (ti % n - si % n) * (1 - 2 * d)
        keep = lag >= jnp.where(ti < n, 1, 0)
        tri = (lag[:n, :n] >= 0).astype(BF16)
        for order in range(RW_CHUNKS_PER_STEP):
            pos = order if d == 0 else RW_CHUNKS_PER_STEP - 1 - order
            rows = slice(pos * n, (pos + 1) * n)
            for bb in range(RW_BATCH_BLOCK):
                lw = lw_ref[bb, rows, :]
                cum = sum(jnp.dot(tri, part, preferred_element_type=F32) for part in _split_bf16(lw, 3))
                e_neg = jnp.exp(-cum)
                ar_all = jnp.concatenate([-kk_ref[bb, rows, :] * jnp.exp(cum - lw),
                                          r_ref[bb, rows, :] * jnp.exp(cum)], axis=0)
                bk_all = jnp.concatenate([be_ref[bb, rows, :] * e_neg, kd_ref[bb, rows, :] * e_neg], axis=0)
                v_all = v_ref[bb, rows, :]
                e_tot = jnp.exp(jnp.sum(lw, axis=0, keepdims=True))
                for h in range(RW_HEADS):
                    cols = slice(h * hd, (h + 1) * hd)
                    items.append(dict(order=order, chain=(bb, d, h), rows=rows, cols=cols, keep=keep,
                                      ar=ar_all[:, cols], bk=bk_all[:, cols], v=v_all[:, cols], decay=e_tot[:, cols]))

    for it in items:
        it["m"] = jnp.where(it["keep"], _mm(it["ar"], it["bk"], NT), 0.0)
    for it, inv in zip(items, _unit_triangular_inverses([it["m"][:n, :n] for it in items])):
        it["inv"] = inv

    state = {}
    for bb in range(RW_BATCH_BLOCK):
        for d in range(2):
            for h in range(RW_HEADS):
                state[(bb, d, h)] = state_ref[bb, d, h]
    for order in range(RW_CHUNKS_PER_STEP):
        group = [it for it in items if it["order"] == order]
        for it in group:
            it["xy"] = (_mm(it["ar"], state[it["chain"]], NT)
                        + _mm(it["m"][:, n:], it["v"], NN))
        for it in group:
            it["u"] = _mm(it["inv"], it["xy"][:n], NN)
        for it in group:
            bb, d, _ = it["chain"]
            y = it["xy"][n:] + _mm(it["m"][n:, :n], it["u"], NN)
            (yf_ref, yb_ref)[d][bb, it["rows"], it["cols"]] = y
        for it in group:
            upd = _mm(jnp.concatenate([it["u"], it["v"]], axis=0), it["bk"], TN)
            state[it["chain"]] = (state[it["chain"]] + upd) * it["decay"]
    for (bb, d, h), st in state.items():
        state_ref[bb, d, h] = st


def _rw_scan(r, v, kk, lw, kd, be):
    b, s, w = r.shape
    rows = RW_CHUNK * RW_CHUNKS_PER_STEP
    nblk = s // rows
    nb = RW_BATCH_BLOCK

    def spec(d, col):
        if d == 0:
            return pl.BlockSpec((nb, rows, w), lambda bi, c: (bi, c, col))
        return pl.BlockSpec((nb, rows, w), lambda bi, c: (bi, nblk - 1 - c, col))

    in_specs = []
    for d in range(2):
        in_specs += [spec(d, 0), spec(d, 0), spec(d, 0), spec(d, d), spec(d, d), spec(d, d)]
    return pl.pallas_call(
        _rw_scan_kernel,
        grid=(b // nb, nblk),
        in_specs=in_specs,
        out_specs=[spec(0, 0), spec(1, 0)],
        out_shape=[jax.ShapeDtypeStruct((b, s, w), F32)] * 2,
        scratch_shapes=[pltpu.VMEM((nb, 2, RW_HEADS, RW_HEAD_DIM, RW_HEAD_DIM), F32)],
        compiler_params=_cparams("parallel", "arbitrary"),
        name="rw_scan",
    )(r, v, kk, lw, kd, be, r, v, kk, lw, kd, be)


def _rw_finish(yf, yb, gate, bonus, gn_g, gn_b, seg):
    y = yf + yb
    mu = _seg_sum(y, seg) * (1.0 / RW_HEAD_DIM)
    yc = y - mu
    var = _seg_sum(yc * yc, seg) * (1.0 / RW_HEAD_DIM)
    yn = yc * lax.rsqrt(var + GN_EPS) * gn_g + gn_b
    return ((yn + bonus) * gate).astype(BF16)


def _merge_kernel(ya_ref, yb_ref, yf_ref, yr_ref, rgate_ref, bonus_ref, gn_g_ref, gn_b_ref, seg_ref,
                  wg_ref, wb_ref, bg_ref, wo_ref, x_ref, g_ref, b_ref, o_ref):
    x = x_ref[...]
    xb = x.astype(BF16)
    yc = _rw_finish(yf_ref[...], yr_ref[...], rgate_ref[...], bonus_ref[...], gn_g_ref[...], gn_b_ref[...],
                    seg_ref[...])
    merged = None
    for n, y in enumerate((ya_ref[...], yb_ref[...], yc)):
        cols = slice(n * D_MODEL, (n + 1) * D_MODEL)
        gate_logits = jnp.dot(xb, wg_ref[:, cols], preferred_element_type=F32) + bg_ref[:, cols]
        branch = jnp.dot(y, wb_ref[n], preferred_element_type=F32)
        term = _sigmoid(gate_logits) * branch
        merged = term if merged is None else merged + term
    mix = jnp.dot(merged.astype(BF16), wo_ref[...], preferred_element_type=F32)
    o_ref[...] = _layer_norm(ALPHA * x + mix, g_ref[...], b_ref[...])


def _merge(ya, yb, yf, yr, rgate, bonus, gn_g, gn_b, w_gate, w_branch, b_gate, w_out, x, g, b):
    n = x.shape[0]
    tm = ROW_TILE
    ytile = pl.BlockSpec((tm, BRANCH_WIDTH), lambda i: (i, 0))
    row = pl.BlockSpec((tm, D_MODEL), lambda i: (i, 0))
    vec = pl.BlockSpec((1, D_MODEL), lambda i: (0, 0))
    hvec = pl.BlockSpec((1, RW_WIDTH), lambda i: (0, 0))
    return pl.pallas_call(
        _merge_kernel,
        grid=(n // tm,),
        in_specs=[ytile, ytile, ytile, ytile, ytile, ytile, hvec, hvec,
                  pl.BlockSpec((SEG_LANES, SEG_LANES), lambda i: (0, 0)),
                  pl.BlockSpec((D_MODEL, 3 * D_MODEL), lambda i: (0, 0)),
                  pl.BlockSpec((3, BRANCH_WIDTH, D_MODEL), lambda i: (0, 0, 0)),
                  pl.BlockSpec((1, 3 * D_MODEL), lambda i: (0, 0)),
                  pl.BlockSpec((D_MODEL, D_MODEL), lambda i: (0, 0)),
                  row, vec, vec],
        out_specs=row,
        out_shape=jax.ShapeDtypeStruct((n, D_MODEL), F32),
        compiler_params=_cparams("parallel"),
        name="merge",
    )(ya, yb, yf, yr, rgate, bonus, gn_g.reshape(1, -1), gn_b.reshape(1, -1), _head_segment_matrix(),
      w_gate, w_branch.astype(BF16), b_gate.reshape(1, -1), w_out.astype(BF16), x, g.reshape(1, -1), b.reshape(1, -1))


def _mlp_kernel(x_ref, wu_ref, wd_ref, g_ref, b_ref, o_ref, ob_ref):
    x = x_ref[...]
    xb = x.astype(BF16)
    acc = None
    for j in range(D_FF // MLP_FF_SLICE):
        ff = slice(j * MLP_FF_SLICE, (j + 1) * MLP_FF_SLICE)
        hid = jnp.maximum(jnp.dot(xb, wu_ref[:, ff], preferred_element_type=F32), 0.0)
        part = jnp.dot((hid * hid).astype(BF16), wd_ref[ff, :], preferred_element_type=F32)
        acc = part if acc is None else acc + part
    y = _layer_norm(ALPHA * x + acc, g_ref[...], b_ref[...])
    o_ref[...] = y
    ob_ref[...] = y.astype(BF16)


def _mlp(x, w_up, w_down, g, b):
    n = x.shape[0]
    tm = MLP_ROW_TILE
    row = pl.BlockSpec((tm, D_MODEL), lambda i: (i, 0))
    vec = pl.BlockSpec((1, D_MODEL), lambda i: (0, 0))
    resident = dict(pipeline_mode=pl.Buffered(1))
    return pl.pallas_call(
        _mlp_kernel,
        grid=(n // tm,),
        in_specs=[row,
                  pl.BlockSpec((D_MODEL, D_FF), lambda i: (0, 0), **resident),
                  pl.BlockSpec((D_FF, D_MODEL), lambda i: (0, 0), **resident),
                  vec, vec],
        out_specs=[row, row],
        out_shape=[jax.ShapeDtypeStruct((n, D_MODEL), F32), jax.ShapeDtypeStruct((n, D_MODEL), BF16)],
        compiler_params=_cparams("parallel"),
        name="mlp",
    )(x, w_up.astype(BF16), w_down.astype(BF16), g.reshape(1, -1), b.reshape(1, -1))


def kernel(x, ln0_g, ln0_b, w_in, pool_w, pool_scale, da_lambda, da_subln_g, rel_bias, rw_mu_prev, rw_mu_next, rw_w0, rw_w2, rw_a0, rw_a2, rw_g2, rw_k_k, rw_k_a, rw_r_k, rw_gn_g, rw_gn_b, w_branch, b_gate, w_out, ln1_g, ln1_b, w_up, w_down, ln2_g, ln2_b):
    b, s, dm = x.shape
    n = b * s
    assert dm == D_MODEL and n % MLP_ROW_TILE == 0 and n % ROW_TILE == 0
    assert s % ATTN_Q_TILE == 0 and s % ROW_TILE == 0 and s % RW_PREP_TILE == 0
    assert b % RW_BATCH_BLOCK == 0 and s % (RW_CHUNK * RW_CHUNKS_PER_STEP) == 0
    c_uqkv = 4 * BRANCH_WIDTH
    c_zr = c_uqkv + RW_SHIFT_WIDTH

    strip = _bias_strip(rel_bias, s)
    xf, xb = _ln0(x.reshape(n, dm), ln0_g, ln0_b)
    for l in range(DEPTH):
        wl = w_in[l].astype(BF16)
        u, q, kv = (t.reshape(b, s, -1) for t in _proj(xb, wl[:, :c_uqkv]))

        y_a = _pool(u, pool_w[l], pool_scale[l])
        lam_init = 0.8 - 0.6 * math.exp(-0.3 * l)
        y_b = _diff_attention(q, kv, strip, da_lambda[l], da_subln_g[l], lam_init)
        r, v, kk, lw, kd, be, gate, bonus = _rw_prep(
            xb.reshape(b, s, dm), wl[:, c_uqkv:c_zr], rw_mu_prev[l], rw_mu_next[l], rw_w0[l], rw_w2[l], rw_a0[l], rw_a2[l], rw_g2[l],
            rw_k_k[l], rw_k_a[l], rw_r_k[l])
        yf, yr = _rw_scan(r, v, kk, lw, kd, be)

        flat = lambda t: t.reshape(n, -1)
        xf = _merge(flat(y_a), flat(y_b), flat(yf), flat(yr), flat(gate), flat(bonus), rw_gn_g[l], rw_gn_b[l],
                    wl[:, c_zr:], w_branch[l], b_gate[l], w_out[l], xf, ln1_g[l], ln1_b[l])
        xf, xb = _mlp(xf, w_up[l], w_down[l], ln2_g[l], ln2_b[l])
    return xf.reshape(b, s, dm)
```

```python
import functools
import math

import numpy as np
import jax
import jax.numpy as jnp
from jax import lax
from jax.experimental import pallas as pl
from jax.experimental.pallas import tpu as pltpu

D_MODEL = 1024
DEPTH = 4
BRANCH_WIDTH = 512
POOL_WINDOWS = (2, 4, 8, 16)
POOL_GROUP_DIM = 128
POOL_PAD = 8
DA_HEAD_DIM = 64
DA_HEADS = 4
REL_BUCKETS = 32
REL_MAX_DIST = 128
RW_WIDTH = 512
RW_HEAD_DIM = 64
RW_HEADS = 8
DECAY_RANK = 64
ICLR_RANK = 64
GATE_RANK = 128
RW_SHIFT_WIDTH = 3 * RW_WIDTH + 2 * DECAY_RANK + 2 * ICLR_RANK + GATE_RANK
GN_EPS = 64e-5
D_FF = 4 * D_MODEL
LN_EPS = 1e-5
ALPHA = (2.0 * DEPTH) ** 0.25
LOG2E = math.log2(math.e)

VMEM_LIMIT_BYTES = 56 * 1024 * 1024

ROW_TILE = 512
MLP_ROW_TILE = 1024
MLP_FF_SLICE = 1024
ATTN_Q_TILE = 256
ATTN_HEADS_PER_STEP = 4
RW_PREP_TILE = 512
RW_HALO = 16
RW_CHUNK = 64
RW_CHUNKS_PER_STEP = 2
RW_BATCH_BLOCK = 2
RW_INV_BASE = 8

F32 = jnp.float32
BF16 = jnp.bfloat16
NN = (((1,), (0,)), ((), ()))
NT = (((1,), (1,)), ((), ()))
TN = (((0,), (0,)), ((), ()))


def _cparams(*sem):
    return pltpu.CompilerParams(dimension_semantics=sem, vmem_limit_bytes=VMEM_LIMIT_BYTES)


def _layer_norm(x, g, b):
    mu = jnp.mean(x, axis=-1, keepdims=True)
    xc = x - mu
    var = jnp.mean(xc * xc, axis=-1, keepdims=True)
    return xc * lax.rsqrt(var + LN_EPS) * g + b


def _split_bf16(x, terms):
    parts = []
    for _ in range(terms):
        p = x.astype(BF16)
        parts.append(p)
        x = x - p.astype(F32)
    return parts


def _hilo_weights(b):
    hi, lo = _split_bf16(b, 2)
    return jnp.concatenate([hi, hi], axis=0), lo


def _mm_hilo(a, b_hihi, b_lo):
    ah, al = _split_bf16(a, 2)
    return (jnp.dot(jnp.concatenate([ah, al], axis=1), b_hihi, preferred_element_type=F32)
            + jnp.dot(ah, b_lo, preferred_element_type=F32))


def _mm(a, b, dims):
    return lax.dot_general(a.astype(BF16), b.astype(BF16), dims, preferred_element_type=F32)


def _ln0_kernel(x_ref, g_ref, b_ref, o_ref, ob_ref):
    y = _layer_norm(x_ref[...], g_ref[...], b_ref[...])
    o_ref[...] = y
    ob_ref[...] = y.astype(BF16)


def _ln0(x2, g, b):
    n = x2.shape[0]
    row = pl.BlockSpec((ROW_TILE, D_MODEL), lambda i: (i, 0))
    vec = pl.BlockSpec((1, D_MODEL), lambda i: (0, 0))
    return pl.pallas_call(
        _ln0_kernel,
        grid=(n // ROW_TILE,),
        in_specs=[row, vec, vec],
        out_specs=[row, row],
        out_shape=[jax.ShapeDtypeStruct((n, D_MODEL), F32), jax.ShapeDtypeStruct((n, D_MODEL), BF16)],
        compiler_params=_cparams("parallel"),
        name="ln0",
    )(x2, g.reshape(1, -1), b.reshape(1, -1))


def _proj_kernel(x_ref, w_ref, u_ref, q_ref, kv_ref):
    x = x_ref[...]
    w = BRANCH_WIDTH
    u_ref[...] = jnp.dot(x, w_ref[:, :w], preferred_element_type=F32)
    q = jnp.dot(x, w_ref[:, w:2 * w], preferred_element_type=F32)
    q_ref[...] = (q * (DA_HEAD_DIM ** -0.5 * LOG2E)).astype(BF16)
    kv_ref[...] = jnp.dot(x, w_ref[:, 2 * w:], preferred_element_type=F32).astype(BF16)


def _proj(xb, w_uqkv):
    n, k = xb.shape
    w = BRANCH_WIDTH

    def out(width):
        return pl.BlockSpec((ROW_TILE, width), lambda i: (i, 0))

    return pl.pallas_call(
        _proj_kernel,
        grid=(n // ROW_TILE,),
        in_specs=[pl.BlockSpec((ROW_TILE, k), lambda i: (i, 0)),
                  pl.BlockSpec((k, 4 * w), lambda i: (0, 0))],
        out_specs=[out(w), out(w), out(2 * w)],
        out_shape=[jax.ShapeDtypeStruct((n, w), F32), jax.ShapeDtypeStruct((n, w), BF16),
                   jax.ShapeDtypeStruct((n, 2 * w), BF16)],
        compiler_params=_cparams("parallel"),
        name="proj_uqkv",
    )(xb, w_uqkv)


def _pool_kernel(u_ref, w_ref, s_ref, o_ref):
    s = u_ref.shape[1]
    pad = POOL_PAD
    t = lax.broadcasted_iota(jnp.int32, (s, POOL_GROUP_DIM), 0)
    te = lax.broadcasted_iota(jnp.int32, (s + pad, POOL_GROUP_DIM), 0)
    zeros = jnp.zeros((pad, POOL_GROUP_DIM), F32)
    for gi, win in enumerate(POOL_WINDOWS):
        left = win // 2
        right = win - 1 - left
        cols = slice(gi * POOL_GROUP_DIM, (gi + 1) * POOL_GROUP_DIM)
        x = u_ref[0, :, cols]
        acc = jnp.concatenate([x, zeros], axis=0)
        span = 1
        while span < win:
            acc = acc + jnp.where(te >= span, pltpu.roll(acc, span, axis=0), 0.0)
            span *= 2
        window = acc[:s] if right == 0 else pltpu.roll(acc, s + pad - right, axis=0)[:s]
        cnt = (jnp.minimum(t + right + 1, s) - jnp.maximum(t - left, 0)).astype(F32)
        mixed = window / cnt - x
        y = jnp.dot(mixed.astype(BF16), w_ref[gi], preferred_element_type=F32)
        o_ref[0, :, cols] = (y * s_ref[:, cols]).astype(BF16)


def _pool(u, pool_w, pool_scale):
    b, s, _ = u.shape
    return pl.pallas_call(
        _pool_kernel,
        grid=(b,),
        in_specs=[pl.BlockSpec((1, s, BRANCH_WIDTH), lambda i: (i, 0, 0)),
                  pl.BlockSpec((len(POOL_WINDOWS), POOL_GROUP_DIM, POOL_GROUP_DIM), lambda i: (0, 0, 0)),
                  pl.BlockSpec((1, BRANCH_WIDTH), lambda i: (0, 0))],
        out_specs=pl.BlockSpec((1, s, BRANCH_WIDTH), lambda i: (i, 0, 0)),
        out_shape=jax.ShapeDtypeStruct((b, s, BRANCH_WIDTH), BF16),
        compiler_params=_cparams("parallel"),
        name="pool",
    )(u, pool_w.astype(BF16), pool_scale.reshape(1, -1))


def _rel_bucket_np(rel):
    half = REL_BUCKETS // 2
    max_exact = half // 2
    n = np.abs(rel)
    nf = np.maximum(n, 1).astype(np.float64)
    large = max_exact + (np.log(nf / max_exact) / math.log(REL_MAX_DIST / max_exact)
                         * (half - max_exact)).astype(np.int32)
    large = np.minimum(large, half - 1)
    return (np.where(rel > 0, half, 0) + np.where(n < max_exact, n, large)).astype(np.int32)


def _bias_strip_kernel(tab_ref, bk_ref, o_ref):
    h = pl.program_id(0)
    bk = bk_ref[...]
    acc = jnp.zeros(bk.shape, F32)
    for j in range(REL_BUCKETS):
        acc = jnp.where(bk == j, tab_ref[j * DA_HEADS + h] * LOG2E, acc)
    o_ref[0] = acc


def _bias_strip(rel_bias, s):
    tq = ATTN_Q_TILE
    w = 2 * s - tq
    rel = np.arange(w)[None, :] - np.arange(tq)[:, None] - (s - tq)
    bucket = jnp.asarray(_rel_bucket_np(rel))
    return pl.pallas_call(
        _bias_strip_kernel,
        grid=(DA_HEADS,),
        in_specs=[pl.BlockSpec(memory_space=pltpu.SMEM),
                  pl.BlockSpec((tq, w), lambda h: (0, 0))],
        out_specs=pl.BlockSpec((1, tq, w), lambda h: (h, 0, 0)),
        out_shape=jax.ShapeDtypeStruct((DA_HEADS, tq, w), F32),
        compiler_params=_cparams("arbitrary"),
        name="bias_strip",
    )(rel_bias.reshape(-1), bucket)


def _attn_kernel(lam0_ref, lamv_ref, q_ref, k_ref, v_ref, strip_ref, g_ref, o_ref):
    tq = q_ref.shape[1]
    s = k_ref.shape[1]
    hw = 2 * DA_HEAD_DIM
    qi = pl.program_id(2)
    nq = pl.num_programs(2)
    lam_init = lam0_ref[0]
    lv = lamv_ref[...]
    lam = (jnp.exp(jnp.sum(lv[0:1] * lv[1:2], axis=-1, keepdims=True))
           - jnp.exp(jnp.sum(lv[2:3] * lv[3:4], axis=-1, keepdims=True)) + lam_init)
    off = pl.multiple_of((nq - 1 - qi) * tq, tq)
    lane = lax.broadcasted_iota(jnp.int32, (tq, hw), 1)

    def logits(h, first_half):
        cols = slice(h * hw, (h + 1) * hw)
        q = q_ref[0, :, cols]
        qm = jnp.where((lane < DA_HEAD_DIM) == first_half, q, jnp.zeros_like(q))
        return (lax.dot_general(qm, k_ref[0, :, cols], NT, preferred_element_type=F32)
                + strip_ref[h, :, pl.ds(off, s)])

    def softmax_times_v(h, lg):
        m = jnp.max(lg, axis=-1, keepdims=True)
        e = jnp.exp2(lg - m)
        l = jnp.sum(e, axis=-1, keepdims=True)
        return jnp.dot(e.astype(BF16), v_ref[0, :, h * hw:(h + 1) * hw], preferred_element_type=F32) / l

    streams = [(h, first) for h in range(ATTN_HEADS_PER_STEP) for first in (True, False)]
    pending = logits(*streams[0])
    outs = []
    for idx, (h, _) in enumerate(streams):
        lg = pending
        if idx + 1 < len(streams):
            pending = logits(*streams[idx + 1])
        outs.append(softmax_times_v(h, lg))
    for h in range(ATTN_HEADS_PER_STEP):
        o = outs[2 * h] - lam * outs[2 * h + 1]
        o = o * lax.rsqrt(jnp.mean(o * o, axis=-1, keepdims=True) + 1e-5) * g_ref[...]
        o_ref[0, :, h * hw:(h + 1) * hw] = (o * (1.0 - lam_init)).astype(BF16)


def _diff_attention(q, kv, strip, da_lambda, subln_g, lam_init):
    b, s, _ = q.shape
    tq = ATTN_Q_TILE
    nh = ATTN_HEADS_PER_STEP
    hw = 2 * DA_HEAD_DIM
    bw = nh * hw
    v_blk0 = BRANCH_WIDTH // bw
    return pl.pallas_call(
        _attn_kernel,
        grid=(b, DA_HEADS // nh, s // tq),
        in_specs=[pl.BlockSpec(memory_space=pltpu.SMEM),
                  pl.BlockSpec((4, DA_HEAD_DIM), lambda bi, hp, qi: (0, 0)),
                  pl.BlockSpec((1, tq, bw), lambda bi, hp, qi: (bi, qi, hp)),
                  pl.BlockSpec((1, s, bw), lambda bi, hp, qi: (bi, 0, hp)),
                  pl.BlockSpec((1, s, bw), lambda bi, hp, qi: (bi, 0, v_blk0 + hp)),
                  pl.BlockSpec((nh, tq, strip.shape[2]), lambda bi, hp, qi: (hp, 0, 0),
                               pipeline_mode=pl.Buffered(1 if nh == DA_HEADS else 2)),
                  pl.BlockSpec((1, hw), lambda bi, hp, qi: (0, 0))],
        out_specs=pl.BlockSpec((1, tq, bw), lambda bi, hp, qi: (bi, qi, hp)),
        out_shape=jax.ShapeDtypeStruct((b, s, BRANCH_WIDTH), BF16),
        compiler_params=_cparams("parallel", "parallel", "arbitrary"),
        name="diff_attn",
    )(jnp.full((1,), lam_init, F32), da_lambda, q, kv, kv, strip, subln_g.reshape(1, -1))


SEG_LANES = 256


def _head_segment_matrix():
    heads = SEG_LANES // RW_HEAD_DIM
    return jnp.asarray(np.kron(np.eye(heads), np.ones((RW_HEAD_DIM, RW_HEAD_DIM))), dtype=BF16)


def _seg_sum(x, seg):
    parts = _split_bf16(x, 2)
    blocks = []
    for c in range(x.shape[1] // SEG_LANES):
        cols = slice(c * SEG_LANES, (c + 1) * SEG_LANES)
        blocks.append(sum(jnp.dot(p[:, cols], seg, preferred_element_type=F32) for p in parts))
    return jnp.concatenate(blocks, axis=1)


def _sigmoid(x):
    return 1.0 / (1.0 + jnp.exp(-x))


def _rw_prep_kernel(x_ref, xp_ref, xn_ref, wz_ref, mup_ref, mun_ref, w0_ref, w2h_ref, w2l_ref, a0_ref, a2h_ref,
                    a2l_ref, g2h_ref, g2l_ref, kk_w_ref, ka_w_ref, rk_w_ref, seg_ref,
                    r_ref, v_ref, kk_ref, lw_ref, kd_ref, be_ref, gate_ref, bonus_ref):
    i = pl.program_id(1)
    n = pl.num_programs(1)
    ts = x_ref.shape[1]
    halo = RW_HALO
    xe = jnp.concatenate([xp_ref[0], x_ref[0], xn_ref[0]], axis=0)
    ze = jnp.dot(xe, wz_ref[...], preferred_element_type=F32)
    z = ze[halo:halo + ts]
    slab = 8
    t = lax.broadcasted_iota(jnp.int32, (slab, 1), 0)
    prev_row = jnp.where(i > 0, ze[halo - 1:halo], 0.0)
    next_row = jnp.where(i < n - 1, ze[halo + ts:halo + ts + 1], 0.0)
    prev = pltpu.roll(z, 1, axis=0)
    prev = jnp.concatenate([jnp.where(t == 0, prev_row, prev[:slab]), prev[slab:]], axis=0)
    nxt = pltpu.roll(z, ts - 1, axis=0)
    nxt = jnp.concatenate([nxt[:ts - slab], jnp.where(t == slab - 1, next_row, nxt[ts - slab:])], axis=0)
    z = z + mup_ref[...] * (prev - z) + mun_ref[...] * (nxt - z)

    w = RW_WIDTH
    r = z[:, 0:w]
    k = z[:, w:2 * w]
    v = z[:, 2 * w:3 * w]
    lw = z[:, 3 * w:3 * w + 2 * DECAY_RANK]
    la = z[:, 3 * w + 2 * DECAY_RANK:3 * w + 2 * DECAY_RANK + 2 * ICLR_RANK]
    lg = z[:, 3 * w + 2 * DECAY_RANK + 2 * ICLR_RANK:]

    x = w0_ref[...] + _mm_hilo(jnp.tanh(lw), w2h_ref[...], w2l_ref[...])
    log_decay = -math.exp(-0.5) / (1.0 + jnp.exp(-x))
    iclr = _sigmoid(a0_ref[...] + _mm_hilo(la, a2h_ref[...], a2l_ref[...]))
    gate = _mm_hilo(_sigmoid(lg), g2h_ref[...], g2l_ref[...])

    seg = seg_ref[...]
    kk = k * kk_w_ref[...]
    kk = kk * lax.rsqrt(jnp.maximum(_seg_sum(kk * kk, seg), 1e-24))
    bonus = _seg_sum(r * k * rk_w_ref[...], seg) * v

    r_ref[0] = r
    v_ref[0] = v
    kk_ref[0] = kk
    lw_ref[0] = log_decay
    gate_ref[0] = gate
    bonus_ref[0] = bonus
    ka = ka_w_ref[...]
    for d in range(2):
        a_d = iclr[:, d * w:(d + 1) * w]
        kd_ref[0, :, d * w:(d + 1) * w] = k * (1.0 + (a_d - 1.0) * ka)
        be_ref[0, :, d * w:(d + 1) * w] = kk * a_d


def _block_diag2(m):
    z = jnp.zeros_like(m[0])
    return jnp.concatenate([jnp.concatenate([m[0], z], axis=1), jnp.concatenate([z, m[1]], axis=1)], axis=0)


def _rw_prep(xb, w_zr, mu_prev, mu_next, w0, w2, a0, a2, g2, k_k, k_a, r_k):
    b, s, dm = xb.shape
    c = w_zr.shape[1]
    ts = RW_PREP_TILE
    w = RW_WIDTH
    seg = _head_segment_matrix()
    halo = RW_HALO
    nhalo = s // halo

    def vec(width):
        return pl.BlockSpec((1, width), lambda bi, i: (0, 0))

    def mat(rows, cols):
        return pl.BlockSpec((rows, cols), lambda bi, i: (0, 0))

    def out(width):
        return pl.BlockSpec((1, ts, width), lambda bi, i: (bi, i, 0))

    shapes = [jax.ShapeDtypeStruct((b, s, width), F32) for width in (w, w, w, 2 * w, 2 * w, 2 * w, w, w)]
    return pl.pallas_call(
        _rw_prep_kernel,
        grid=(b, s // ts),
        in_specs=[pl.BlockSpec((1, ts, dm), lambda bi, i: (bi, i, 0)),
                  pl.BlockSpec((1, halo, dm), lambda bi, i: (bi, jnp.maximum(i * (ts // halo) - 1, 0), 0)),
                  pl.BlockSpec((1, halo, dm), lambda bi, i: (bi, jnp.minimum((i + 1) * (ts // halo), nhalo - 1), 0)),
                  mat(dm, c), vec(c), vec(c), vec(2 * w), mat(4 * DECAY_RANK, 2 * w), mat(2 * DECAY_RANK, 2 * w),
                  vec(2 * w), mat(4 * ICLR_RANK, 2 * w), mat(2 * ICLR_RANK, 2 * w),
                  mat(2 * GATE_RANK, w), mat(GATE_RANK, w), vec(w), vec(w), vec(w), mat(SEG_LANES, SEG_LANES)],
        out_specs=[out(w), out(w), out(w), out(2 * w), out(2 * w), out(2 * w), out(w), out(w)],
        out_shape=shapes,
        compiler_params=_cparams("parallel", "arbitrary"),
        name="rw_prep",
    )(xb, xb, xb, w_zr, mu_prev.reshape(1, -1), mu_next.reshape(1, -1), w0.reshape(1, -1),
      *_hilo_weights(_block_diag2(w2)),
      a0.reshape(1, -1), *_hilo_weights(_block_diag2(a2)), *_hilo_weights(g2),
      k_k.reshape(1, -1), k_a.reshape(1, -1), r_k.reshape(1, -1), seg)


def _unit_triangular_inverses(mats):
    n = mats[0].shape[0]
    ti = lax.broadcasted_iota(jnp.int32, (n, n), 0)
    si = lax.broadcasted_iota(jnp.int32, (n, n), 1)

    def same_block(m):
        return (ti // m) == (si // m)

    mm = functools.partial(_mm, dims=NN)
    eye = (ti == si).astype(F32)
    base = same_block(RW_INV_BASE)
    diag = [jnp.where(base, a, 0.0) for a in mats]
    power = [mm(a, a) for a in diag]
    inv = [mm(eye + a, eye + p) for a, p in zip(diag, power)]
    m = 4
    while m < RW_INV_BASE:
        power = [mm(p, p) for p in power]
        inv = [mm(t, eye + p) for t, p in zip(inv, power)]
        m *= 2
    m = RW_INV_BASE
    while m < n:
        ring = jnp.logical_and(same_block(2 * m), jnp.logical_not(same_block(m)))
        tmp = [mm(jnp.where(ring, a, 0.0), t) for a, t in zip(mats, inv)]
        inv = [t + mm(t, x) for t, x in zip(inv, tmp)]
        m *= 2
    return inv


def _rw_scan_kernel(rf_ref, vf_ref, kkf_ref, lwf_ref, kdf_ref, bef_ref,
                    rb_ref, vb_ref, kkb_ref, lwb_ref, kdb_ref, beb_ref, yf_ref, yb_ref, state_ref):
    @pl.when(pl.program_id(1) == 0)
    def _():
        state_ref[...] = jnp.zeros_like(state_ref)

    n = RW_CHUNK
    hd = RW_HEAD_DIM
    ti = lax.broadcasted_iota(jnp.int32, (2 * n, 2 * n), 0)
    si = lax.broadcasted_iota(jnp.int32, (2 * n, 2 * n), 1)

    items = []
    for d, (r_ref, v_ref, kk_ref, lw_ref, kd_ref, be_ref) in enumerate((
            (rf_ref, vf_ref, kkf_ref, lwf_ref, kdf_ref, bef_ref),
            (rb_ref, vb_ref, kkb_ref, lwb_ref, kdb_ref, beb_ref))):
        lag = (ti % n - si % n) * (1 - 2 * d)
        keep = lag >= jnp.where(ti < n, 1, 0)
        tri = (lag[:n, :n] >= 0).astype(BF16)
        for order in range(RW_CHUNKS_PER_STEP):
            pos = order if d == 0 else RW_CHUNKS_PER_STEP - 1 - order
            rows = slice(pos * n, (pos + 1) * n)
            for bb in range(RW_BATCH_BLOCK):
                lw = lw_ref[bb, rows, :]
                cum = sum(jnp.dot(tri, part, preferred_element_type=F32) for part in _split_bf16(lw, 3))
                e_neg = jnp.exp(-cum)
                ar_all = jnp.concatenate([-kk_ref[bb, rows, :] * jnp.exp(cum - lw),
                                          r_ref[bb, rows, :] * jnp.exp(cum)], axis=0)
                bk_all = jnp.concatenate([be_ref[bb, rows, :] * e_neg, kd_ref[bb, rows, :] * e_neg], axis=0)
                v_all = v_ref[bb, rows, :]
                e_tot = jnp.exp(jnp.sum(lw, axis=0, keepdims=True))
                for h in range(RW_HEADS):
                    cols = slice(h * hd, (h + 1) * hd)
                    items.append(dict(order=order, chain=(bb, d, h), rows=rows, cols=cols, keep=keep,
                                      ar=ar_all[:, cols], bk=bk_all[:, cols], v=v_all[:, cols], decay=e_tot[:, cols]))

    for it in items:
        it["m"] = jnp.where(it["keep"], _mm(it["ar"], it["bk"], NT), 0.0)
    for it, inv in zip(items, _unit_triangular_inverses([it["m"][:n, :n] for it in items])):
        it["inv"] = inv

    state = {}
    for bb in range(RW_BATCH_BLOCK):
        for d in range(2):
            for h in range(RW_HEADS):
                state[(bb, d, h)] = state_ref[bb, d, h]
    for order in range(RW_CHUNKS_PER_STEP):
        group = [it for it in items if it["order"] == order]
        for it in group:
            it["xy"] = (_mm(it["ar"], state[it["chain"]], NT)
                        + _mm(it["m"][:, n:], it["v"], NN))
        for it in group:
            it["u"] = _mm(it["inv"], it["xy"][:n], NN)
        for it in group:
            bb, d, _ = it["chain"]
            y = it["xy"][n:] + _mm(it["m"][n:, :n], it["u"], NN)
            (yf_ref, yb_ref)[d][bb, it["rows"], it["cols"]] = y
        for it in group:
            upd = _mm(jnp.concatenate([it["u"], it["v"]], axis=0), it["bk"], TN)
            state[it["chain"]] = (state[it["chain"]] + upd) * it["decay"]
    for (bb, d, h), st in state.items():
        state_ref[bb, d, h] = st


def _rw_scan(r, v, kk, lw, kd, be):
    b, s, w = r.shape
    rows = RW_CHUNK * RW_CHUNKS_PER_STEP
    nblk = s // rows
    nb = RW_BATCH_BLOCK

    def spec(d, col):
        if d == 0:
            return pl.BlockSpec((nb, rows, w), lambda bi, c: (bi, c, col))
        return pl.BlockSpec((nb, rows, w), lambda bi, c: (bi, nblk - 1 - c, col))

    in_specs = []
    for d in range(2):
        in_specs += [spec(d, 0), spec(d, 0), spec(d, 0), spec(d, d), spec(d, d), spec(d, d)]
    return pl.pallas_call(
        _rw_scan_kernel,
        grid=(b // nb, nblk),
        in_specs=in_specs,
        out_specs=[spec(0, 0), spec(1, 0)],
        out_shape=[jax.ShapeDtypeStruct((b, s, w), F32)] * 2,
        scratch_shapes=[pltpu.VMEM((nb, 2, RW_HEADS, RW_HEAD_DIM, RW_HEAD_DIM), F32)],
        compiler_params=_cparams("parallel", "arbitrary"),
        name="rw_scan",
    )(r, v, kk, lw, kd, be, r, v, kk, lw, kd, be)


def _rw_finish(yf, yb, gate, bonus, gn_g, gn_b, seg):
    y = yf + yb
    mu = _seg_sum(y, seg) * (1.0 / RW_HEAD_DIM)
    yc = y - mu
    var = _seg_sum(yc * yc, seg) * (1.0 / RW_HEAD_DIM)
    yn = yc * lax.rsqrt(var + GN_EPS) * gn_g + gn_b
    return ((yn + bonus) * gate).astype(BF16)


def _merge_kernel(ya_ref, yb_ref, yf_ref, yr_ref, rgate_ref, bonus_ref, gn_g_ref, gn_b_ref, seg_ref,
                  wg_ref, wb_ref, bg_ref, wo_ref, x_ref, g_ref, b_ref, o_ref):
    x = x_ref[...]
    xb = x.astype(BF16)
    yc = _rw_finish(yf_ref[...], yr_ref[...], rgate_ref[...], bonus_ref[...], gn_g_ref[...], gn_b_ref[...],
                    seg_ref[...])
    merged = None
    for n, y in enumerate((ya_ref[...], yb_ref[...], yc)):
        cols = slice(n * D_MODEL, (n + 1) * D_MODEL)
        gate_logits = jnp.dot(xb, wg_ref[:, cols], preferred_element_type=F32) + bg_ref[:, cols]
        branch = jnp.dot(y, wb_ref[n], preferred_element_type=F32)
        term = _sigmoid(gate_logits) * branch
        merged = term if merged is None else merged + term
    mix = jnp.dot(merged.astype(BF16), wo_ref[...], preferred_element_type=F32)
    o_ref[...] = _layer_norm(ALPHA * x + mix, g_ref[...], b_ref[...])


def _merge(ya, yb, yf, yr, rgate, bonus, gn_g, gn_b, w_gate, w_branch, b_gate, w_out, x, g, b):
    n = x.shape[0]
    tm = ROW_TILE
    ytile = pl.BlockSpec((tm, BRANCH_WIDTH), lambda i: (i, 0))
    row = pl.BlockSpec((tm, D_MODEL), lambda i: (i, 0))
    vec = pl.BlockSpec((1, D_MODEL), lambda i: (0, 0))
    hvec = pl.BlockSpec((1, RW_WIDTH), lambda i: (0, 0))
    return pl.pallas_call(
        _merge_kernel,
        grid=(n // tm,),
        in_specs=[ytile, ytile, ytile, ytile, ytile, ytile, hvec, hvec,
                  pl.BlockSpec((SEG_LANES, SEG_LANES), lambda i: (0, 0)),
                  pl.BlockSpec((D_MODEL, 3 * D_MODEL), lambda i: (0, 0)),
                  pl.BlockSpec((3, BRANCH_WIDTH, D_MODEL), lambda i: (0, 0, 0)),
                  pl.BlockSpec((1, 3 * D_MODEL), lambda i: (0, 0)),
                  pl.BlockSpec((D_MODEL, D_MODEL), lambda i: (0, 0)),
                  row, vec, vec],
        out_specs=row,
        out_shape=jax.ShapeDtypeStruct((n, D_MODEL), F32),
        compiler_params=_cparams("parallel"),
        name="merge",
    )(ya, yb, yf, yr, rgate, bonus, gn_g.reshape(1, -1), gn_b.reshape(1, -1), _head_segment_matrix(),
      w_gate, w_branch.astype(BF16), b_gate.reshape(1, -1), w_out.astype(BF16), x, g.reshape(1, -1), b.reshape(1, -1))


def _mlp_kernel(x_ref, wu_ref, wd_ref, g_ref, b_ref, o_ref, ob_ref):
    x = x_ref[...]
    xb = x.astype(BF16)
    acc = None
    for j in range(D_FF // MLP_FF_SLICE):
        ff = slice(j * MLP_FF_SLICE, (j + 1) * MLP_FF_SLICE)
        hid = jnp.maximum(jnp.dot(xb, wu_ref[:, ff], preferred_element_type=F32), 0.0)
        part = jnp.dot((hid * hid).astype(BF16), wd_ref[ff, :], preferred_element_type=F32)
        acc = part if acc is None else acc + part
    y = _layer_norm(ALPHA * x + acc, g_ref[...], b_ref[...])
    o_ref[...] = y
    ob_ref[...] = y.astype(BF16)


def _mlp(x, w_up, w_down, g, b):
    n = x.shape[0]
    tm = MLP_ROW_TILE
    row = pl.BlockSpec((tm, D_MODEL), lambda i: (i, 0))
    vec = pl.BlockSpec((1, D_MODEL), lambda i: (0, 0))
    resident = dict(pipeline_mode=pl.Buffered(1))
    return pl.pallas_call(
        _mlp_kernel,
        grid=(n // tm,),
        in_specs=[row,
                  pl.BlockSpec((D_MODEL, D_FF), lambda i: (0, 0), **resident),
                  pl.BlockSpec((D_FF, D_MODEL), lambda i: (0, 0), **resident),
                  vec, vec],
        out_specs=[row, row],
        out_shape=[jax.ShapeDtypeStruct((n, D_MODEL), F32), jax.ShapeDtypeStruct((n, D_MODEL), BF16)],
        compiler_params=_cparams("parallel"),
        name="mlp",
    )(x, w_up.astype(BF16), w_down.astype(BF16), g.reshape(1, -1), b.reshape(1, -1))


def kernel(x, ln0_g, ln0_b, w_in, pool_w, pool_scale, da_lambda, da_subln_g, rel_bias, rw_mu_prev, rw_mu_next, rw_w0, rw_w2, rw_a0, rw_a2, rw_g2, rw_k_k, rw_k_a, rw_r_k, rw_gn_g, rw_gn_b, w_branch, b_gate, w_out, ln1_g, ln1_b, w_up, w_down, ln2_g, ln2_b):
    b, s, dm = x.shape
    n = b * s
    assert dm == D_MODEL and n % MLP_ROW_TILE == 0 and n % ROW_TILE == 0
    assert s % ATTN_Q_TILE == 0 and s % ROW_TILE == 0 and s % RW_PREP_TILE == 0
    assert b % RW_BATCH_BLOCK == 0 and s % (RW_CHUNK * RW_CHUNKS_PER_STEP) == 0
    c_uqkv = 4 * BRANCH_WIDTH
    c_zr = c_uqkv + RW_SHIFT_WIDTH

    strip = _bias_strip(rel_bias, s)
    xf, xb = _ln0(x.reshape(n, dm), ln0_g, ln0_b)
    for l in range(DEPTH):
        wl = w_in[l].astype(BF16)
        u, q, kv = (t.reshape(b, s, -1) for t in _proj(xb, wl[:, :c_uqkv]))

        y_a = _pool(u, pool_w[l], pool_scale[l])
        lam_init = 0.8 - 0.6 * math.exp(-0.3 * l)
        y_b = _diff_attention(q, kv, strip, da_lambda[l], da_subln_g[l], lam_init)
        r, v, kk, lw, kd, be, gate, bonus = _rw_prep(
            xb.reshape(b, s, dm), wl[:, c_uqkv:c_zr], rw_mu_prev[l], rw_mu_next[l], rw_w0[l], rw_w2[l], rw_a0[l], rw_a2[l], rw_g2[l],
            rw_k_k[l], rw_k_a[l], rw_r_k[l])
        yf, yr = _rw_scan(r, v, kk, lw, kd, be)

        flat = lambda t: t.reshape(n, -1)
        xf = _merge(flat(y_a), flat(y_b), flat(yf), flat(yr), flat(gate), flat(bonus), rw_gn_g[l], rw_gn_b[l],
                    wl[:, c_zr:], w_branch[l], b_gate[l], w_out[l], xf, ln1_g[l], ln1_b[l])
        xf, xb = _mlp(xf, w_up[l], w_down[l], ln2_g[l], ln2_b[l])
    return xf.reshape(b, s, dm)
```
